```python
import math
import jax, jax.numpy as jnp
from jax import lax
import numpy as np

D_MODEL = 1024
BATCH = 8
SEQ = 4096
DEPTH = 2
DEC_BATCH = 2
DEC_SEQ = 8192
PAST_LEN = 128

N_META = 16
GRID_W = 64
QBLOCK = 128
ROPE_THETA = 10000.0
EPS = 1e-6
HA = 4
DA = 64
VA = 2 * DA
HB = 4
Q_LORA = 256
KV_LORA = 128
DB_NOPE = 64
DB_ROPE = 32
DB_V = 64
HC = 4
HC_KV = 2
DC = 64
IN_SPLITS = (HA * 2 * DA, HA * 2 * DA, HA * VA, Q_LORA, KV_LORA, DB_ROPE, HC * DC, HC_KV * DC, HC_KV * DC)
IN_WIDTH = sum(IN_SPLITS)
MIX_WIDTH = HA * VA + HB * DB_V + HC * DC
N_EXPERTS = 16
EC_FACTOR = 2
EXPERT_FF = 2816

kernel_name = 'hymba_diff_mla_axialgqa_expert_choice_encoder'


def _rms(x, g):
    xf = x.astype(jnp.float32)
    y = xf * lax.rsqrt(jnp.mean(xf * xf, axis=-1, keepdims=True) + EPS)
    return (y * g.astype(jnp.float32)).astype(x.dtype)


def _rope(x, pos):
    half = x.shape[-1] // 2
    inv = ROPE_THETA ** (-jnp.arange(half, dtype=jnp.float32) / half)
    ang = pos.astype(jnp.float32)[:, None] * inv[None, :]
    shape = (1, x.shape[1]) + (1,) * (x.ndim - 3) + (half,)
    cos = jnp.cos(ang).reshape(shape)
    sin = jnp.sin(ang).reshape(shape)
    xf = x.astype(jnp.float32)
    x1, x2 = xf[..., :half], xf[..., half:]
    return jnp.concatenate([x1 * cos - x2 * sin, x2 * cos + x1 * sin], axis=-1).astype(x.dtype)


def _axial_rope(x, row, col):
    h = x.shape[-1] // 2
    return jnp.concatenate([_rope(x[..., :h], row), _rope(x[..., h:], col)], axis=-1)


def _softmax32(s):
    return jax.nn.softmax(s.astype(jnp.float32), axis=-1)


def _sweep_queries(fn, q):
    b, l = q.shape[:2]
    s = l - N_META
    out_meta = fn(q[:, :N_META])
    qr = q[:, N_META:].reshape((b, s // QBLOCK, QBLOCK) + q.shape[2:])
    out = lax.map(fn, jnp.moveaxis(qr, 1, 0))
    out = jnp.moveaxis(out, 0, 1).reshape((b, s) + out.shape[3:])
    return jnp.concatenate([out_meta, out], axis=1)


def _mixer(u, pos, row, col, layer, w_in, w_out, g_qa, g_ka, lam_q1, lam_k1, lam_q2, lam_k2, g_suba,
           g_cq, w_uq, g_ckv, w_ukv, g_qb, g_kb, g_qc, g_kc):
    b, L, _ = u.shape
    z = u @ w_in
    cuts = np.cumsum(IN_SPLITS)[:-1].tolist()
    za_q, za_k, za_v, z_cq, z_ckv, z_kr, z_qc, z_kc, z_vc = jnp.split(z, cuts, axis=-1)

    qa = _rope(_rms(za_q.reshape(b, L, HA, 2, DA), g_qa), pos)
    ka = _rope(_rms(za_k.reshape(b, L, HA, 2, DA), g_ka), pos)
    va = za_v.reshape(b, L, HA, VA)
    lam_init = 0.8 - 0.6 * math.exp(-0.3 * layer)
    lam = (jnp.exp(jnp.sum(lam_q1.astype(jnp.float32) * lam_k1.astype(jnp.float32)))
           - jnp.exp(jnp.sum(lam_q2.astype(jnp.float32) * lam_k2.astype(jnp.float32))) + lam_init)
    scale_a = DA ** -0.5

    def diff_block(qblk):
        p = _softmax32(jnp.einsum('bqhmd,bkhmd->bmhqk', qblk, ka) * scale_a)
        a = p[:, 0] - lam * p[:, 1]
        return jnp.einsum('bhqk,bkhe->bqhe', a.astype(va.dtype), va)

    oa = _sweep_queries(diff_block, qa)
    oa = (_rms(oa, g_suba) * (1.0 - lam_init)).reshape(b, L, HA * VA)

    qb = (_rms(z_cq, g_cq) @ w_uq).reshape(b, L, HB, DB_NOPE + DB_ROPE)
    kv = (_rms(z_ckv, g_ckv) @ w_ukv).reshape(b, L, HB, DB_NOPE + DB_V)
    kb = jnp.concatenate([kv[..., :DB_NOPE],
                          jnp.broadcast_to(z_kr[:, :, None, :], (b, L, HB, DB_ROPE))], axis=-1)
    vb = kv[..., DB_NOPE:]
    qb = _rms(qb, g_qb)
    kb = _rms(kb, g_kb)
    qb = jnp.concatenate([qb[..., :DB_NOPE], _rope(qb[..., DB_NOPE:], pos)], axis=-1)
    kb = jnp.concatenate([kb[..., :DB_NOPE], _rope(kb[..., DB_NOPE:], pos)], axis=-1)
    scale_b = (DB_NOPE + DB_ROPE) ** -0.5

    def mla_block(qblk):
        p = _softmax32(jnp.einsum('bqhd,bkhd->bhqk', qblk, kb) * scale_b)
        return jnp.einsum('bhqk,bkhd->bqhd', p.astype(vb.dtype), vb)

    ob = _sweep_queries(mla_block, qb).reshape(b, L, HB * DB_V)

    qc = _axial_rope(_rms(z_qc.reshape(b, L, HC_KV, HC // HC_KV, DC), g_qc), row, col)
    kc = _axial_rope(_rms(z_kc.reshape(b, L, HC_KV, DC), g_kc), row, col)
    vc = z_vc.reshape(b, L, HC_KV, DC)
    scale_c = DC ** -0.5

    def gqa_block(qblk):
        p = _softmax32(jnp.einsum('bqhgd,bkhd->bhgqk', qblk, kc) * scale_c)
        return jnp.einsum('bhgqk,bkhd->bqhgd', p.astype(vc.dtype), vc)

    oc = _sweep_queries(gqa_block, qc).reshape(b, L, HC * DC)

    return jnp.concatenate([oa, ob, oc], axis=-1) @ w_out


def _expert_choice_ffn(h, ln, w_router, w_gate, w_up, w_down):
    b, l, d = h.shape
    n = b * l
    cap = EC_FACTOR * n // N_EXPERTS
    xt = _rms(h, ln).reshape(n, d)
    aff = jax.nn.softmax((xt @ w_router).astype(jnp.float32), axis=-1)
    gate, idx = lax.top_k(aff.T, cap)
    xe = xt[idx]
    hid = jax.nn.silu(jnp.einsum('ecd,edf->ecf', xe, w_gate)) * jnp.einsum('ecd,edf->ecf', xe, w_up)
    ye = jnp.einsum('ecf,efd->ecd', hid, w_down) * gate[..., None].astype(h.dtype)
    y = jnp.zeros((n, d), h.dtype).at[idx.reshape(-1)].add(ye.reshape(-1, d))
    return y.reshape(b, l, d)


def _encode(x, meta_tokens, ln_mix, w_in, w_out, g_qa, g_ka, lam_q1, lam_k1, lam_q2, lam_k2, g_suba,
            g_cq, w_uq, g_ckv, w_ukv, g_qb, g_kb, g_qc, g_kc, ln_ffn, w_router, w_gate, w_up, w_down):
    b, s, d = x.shape
    h = jnp.concatenate([jnp.broadcast_to(meta_tokens.astype(x.dtype)[None], (b, N_META, d)), x], axis=1)
    L = s + N_META
    pos = jnp.arange(L, dtype=jnp.int32)
    rows = s // GRID_W
    row = jnp.concatenate([-jnp.ones((N_META,), jnp.int32), jnp.repeat(jnp.arange(rows, dtype=jnp.int32), GRID_W)])
    col = jnp.concatenate([jnp.arange(N_META, dtype=jnp.int32), jnp.tile(jnp.arange(GRID_W, dtype=jnp.int32), rows)])
    for l in range(DEPTH):
        h = h + _mixer(_rms(h, ln_mix[l]), pos, row, col, l, w_in[l], w_out[l], g_qa[l], g_ka[l],
                       lam_q1[l], lam_k1[l], lam_q2[l], lam_k2[l], g_suba[l], g_cq[l], w_uq[l],
                       g_ckv[l], w_ukv[l], g_qb[l], g_kb[l], g_qc[l], g_kc[l])
        h = h + _expert_choice_ffn(h, ln_ffn[l], w_router[l], w_gate[l], w_up[l], w_down[l])
    return h[:, N_META:]


def setup_inputs(seed: int = 0) -> dict:
    key = jax.random.key(seed)
    ks = jax.random.split(key, 32)
    f32 = jnp.float32

    def nrm(k, shape, scale):
        return jax.random.normal(k, shape, f32) * scale

    def gain(k, shape):
        return 1.0 + 0.02 * jax.random.normal(k, shape, f32)

    return {
        'x_prompt': nrm(ks[0], (BATCH, SEQ, D_MODEL), 1.0),
        'x_sample': nrm(ks[1], (DEC_BATCH, DEC_SEQ, D_MODEL), 1.0),
        'meta_tokens': nrm(ks[2], (N_META, D_MODEL), 1.0),
        'ln_mix': gain(ks[3], (DEPTH, D_MODEL)),
        'w_in': nrm(ks[4], (DEPTH, D_MODEL, IN_WIDTH), D_MODEL ** -0.5),
        'w_out': nrm(ks[5], (DEPTH, MIX_WIDTH, D_MODEL), MIX_WIDTH ** -0.5),
        'g_qa': gain(ks[6], (DEPTH, DA)),
        'g_ka': gain(ks[7], (DEPTH, DA)),
        'lam_q1': nrm(ks[8], (DEPTH, DA), 0.1),
        'lam_k1': nrm(ks[9], (DEPTH, DA), 0.1),
        'lam_q2': nrm(ks[10], (DEPTH, DA), 0.1),
        'lam_k2': nrm(ks[11], (DEPTH, DA), 0.1),
        'g_suba': gain(ks[12], (DEPTH, VA)),
        'g_cq': gain(ks[13], (DEPTH, Q_LORA)),
        'w_uq': nrm(ks[14], (DEPTH, Q_LORA, HB * (DB_NOPE + DB_ROPE)), Q_LORA ** -0.5),
        'g_ckv': gain(ks[15], (DEPTH, KV_LORA)),
        'w_ukv': nrm(ks[16], (DEPTH, KV_LORA, HB * (DB_NOPE + DB_V)), KV_LORA ** -0.5),
        'g_qb': gain(ks[17], (DEPTH, DB_NOPE + DB_ROPE)),
        'g_kb': gain(ks[18], (DEPTH, DB_NOPE + DB_ROPE)),
        'g_qc': gain(ks[19], (DEPTH, DC)),
        'g_kc': gain(ks[20], (DEPTH, DC)),
        'ln_ffn': gain(ks[21], (DEPTH, D_MODEL)),
        'w_router': nrm(ks[22], (DEPTH, D_MODEL, N_EXPERTS), D_MODEL ** -0.5),
        'w_gate': nrm(ks[23], (DEPTH, N_EXPERTS, D_MODEL, EXPERT_FF), D_MODEL ** -0.5),
        'w_up': nrm(ks[24], (DEPTH, N_EXPERTS, D_MODEL, EXPERT_FF), D_MODEL ** -0.5),
        'w_down': nrm(ks[25], (DEPTH, N_EXPERTS, EXPERT_FF, D_MODEL), EXPERT_FF ** -0.5),
    }


def reference(x_prompt, x_sample, meta_tokens, ln_mix, w_in, w_out, g_qa, g_ka, lam_q1, lam_k1, lam_q2,
              lam_k2, g_suba, g_cq, w_uq, g_ckv, w_ukv, g_qb, g_kb, g_qc, g_kc, ln_ffn, w_router,
              w_gate, w_up, w_down):
    weights = (meta_tokens, ln_mix, w_in, w_out, g_qa, g_ka, lam_q1, lam_k1, lam_q2, lam_k2, g_suba,
               g_cq, w_uq, g_ckv, w_ukv, g_qb, g_kb, g_qc, g_kc, ln_ffn, w_router, w_gate, w_up, w_down)
    y_prompt = _encode(x_prompt, *weights)
    y_sample = _encode(x_sample, *weights)
    return (y_prompt, y_sample)
```

```python
import functools
import math

import jax
import jax.numpy as jnp
import numpy as np
from jax import lax
from jax.experimental import pallas as pl
from jax.experimental.pallas import tpu as pltpu

F32 = jnp.float32
BF16 = jnp.bfloat16
I32 = jnp.int32

D_MODEL = 1024
N_META = 16
GRID_W = 64
ROPE_THETA = 10000.0
EPS = 1e-6
HA, DA, VA = 4, 64, 128
HB, Q_LORA, KV_LORA, DB_NOPE, DB_ROPE, DB_V = 4, 256, 128, 64, 32, 64
HC, HC_KV, DC = 4, 2, 64
N_EXPERTS = 16
EC_FACTOR = 2
EXPERT_FF = 2816
LOG2E = math.log2(math.e)

LANES = 128
SEQ_PAD = 128
ZW = 2560
NQ, NK, NV, NU = 10, 9, 7, 8
FF_CHUNK = 256
COMBINE_TILE = 256
NEG = -1e30
VMEM_LIMIT = 56 * 1024 * 1024


def _cparams(sem):
    return pltpu.CompilerParams(dimension_semantics=sem, vmem_limit_bytes=VMEM_LIMIT)


def _seg_rms(x, bd, inv_n):
    ss = jnp.dot((x * x).astype(BF16), bd, preferred_element_type=F32)
    return x * lax.rsqrt(ss * inv_n + EPS)


def _in_proj_kernel(h_ref, ln_ref, win_ref, wuq_ref, wkk_ref, wkv_ref, gv_ref, gcq_ref, rope_ref, bd_ref,
                    q_ref, k_ref, v_ref):
    h = h_ref[...]
    ms = jnp.mean(h * h, axis=-1, keepdims=True)
    u = (h * lax.rsqrt(ms + EPS) * ln_ref[...]).astype(BF16)
    z = jnp.dot(u, win_ref[...], preferred_element_type=F32)
    bd64, bd128, ones256 = bd_ref[0], bd_ref[1], bd_ref[2]

    def rope(x, t, sh):
        return (x * rope_ref[3 * t] + pltpu.roll(x, LANES - sh, 1) * rope_ref[3 * t + 1]
                + pltpu.roll(x, sh, 1) * rope_ref[3 * t + 2])

    def blk(x, j):
        return x[:, j * LANES:(j + 1) * LANES]

    for s in range(2):
        xq = _seg_rms(z[:, s * 256:(s + 1) * 256], bd64, 1.0 / DA)
        xk = _seg_rms(z[:, 512 + s * 256:512 + (s + 1) * 256], bd64, 1.0 / DA)
        for j in range(2):
            q_ref[2 * s + j] = rope(blk(xq, j) * gv_ref[0:1, :], 0, 32).astype(BF16)
            k_ref[2 * s + j] = rope(blk(xk, j) * gv_ref[1:2, :], 0, 32).astype(BF16)
    for j in range(4):
        v_ref[j] = blk(z, 8 + j).astype(BF16)

    cq = (_seg_rms(z[:, 1536:1792], ones256, 1.0 / Q_LORA) * gcq_ref[...]).astype(BF16)
    qb = jnp.dot(cq, wuq_ref[...], preferred_element_type=F32)
    ckv = (_seg_rms(z[:, 1792:1920], bd128[:LANES, :LANES], 1.0 / KV_LORA) * gv_ref[6:7, :]).astype(BF16)
    kk = jnp.dot(ckv, wkk_ref[...], preferred_element_type=F32)
    vb = jnp.dot(ckv, wkv_ref[...], preferred_element_type=F32)
    kr = pltpu.roll(z[:, 2432:2560], 64, 1)
    for s in range(2):
        xq = _seg_rms(qb[:, s * 256:(s + 1) * 256], bd128, 1.0 / (DB_NOPE + DB_ROPE))
        kraw = kk[:, s * 256:(s + 1) * 256] + jnp.concatenate([kr, kr], axis=1)
        xk = _seg_rms(kraw, bd128, 1.0 / (DB_NOPE + DB_ROPE))
        for j in range(2):
            q_ref[4 + 2 * s + j] = rope(blk(xq, j) * gv_ref[2:3, :], 1, 16).astype(BF16)
            k_ref[4 + 2 * s + j] = rope(blk(xk, j) * gv_ref[3:4, :], 1, 16).astype(BF16)
        v_ref[4 + s] = vb[:, s * LANES:(s + 1) * LANES].astype(BF16)

    xq = _seg_rms(z[:, 1920:2176], bd64, 1.0 / DC)
    for j in range(2):
        q_ref[8 + j] = rope(blk(xq, j) * gv_ref[4:5, :], 2, 16).astype(BF16)
    xk = _seg_rms(z[:, 2176:2304], bd64[:LANES, :LANES], 1.0 / DC)
    k_ref[8] = rope(xk * gv_ref[5:6, :], 2, 16).astype(BF16)
    v_ref[6] = z[:, 2304:2432].astype(BF16)


def _in_proj(h, ln, win, wuq, wkk, wkv, gv, gcq, rope, bd, *, tm, l_pad):
    n_pad = h.shape[0]
    tiles_per_seq = l_pad // tm
    full = lambda shape: pl.BlockSpec(shape, lambda i: (0,) * len(shape))
    return pl.pallas_call(
        _in_proj_kernel,
        grid=(n_pad // tm,),
        in_specs=[
            pl.BlockSpec((tm, D_MODEL), lambda i: (i, 0)),
            full((1, D_MODEL)), full((D_MODEL, ZW)), full((Q_LORA, 512)), full((KV_LORA, 512)),
            full((KV_LORA, 256)), full((8, LANES)), full((1, Q_LORA)),
            pl.BlockSpec((9, tm, LANES), lambda i: (0, i % tiles_per_seq, 0)),
            full((3, 256, 256)),
        ],
        out_specs=[
            pl.BlockSpec((NQ, tm, LANES), lambda i: (0, i, 0)),
            pl.BlockSpec((NK, tm, LANES), lambda i: (0, i, 0)),
            pl.BlockSpec((NV, tm, LANES), lambda i: (0, i, 0)),
        ],
        out_shape=[
            jax.ShapeDtypeStruct((NQ, n_pad, LANES), BF16),
            jax.ShapeDtypeStruct((NK, n_pad, LANES), BF16),
            jax.ShapeDtypeStruct((NV, n_pad, LANES), BF16),
        ],
        compiler_params=_cparams(("parallel",)),
        name="in_proj",
    )(h, ln, win, wuq, wkk, wkv, gv, gcq, rope, bd)


def _attn_kernel(tbl_ref, q0_ref, q1_ref, k0_ref, k1_ref, v_ref, up_ref, o_ref, *, n_main, tk, tq):
    del tbl_ref
    up = up_ref[0]
    q0 = (q0_ref[0].astype(F32) * up[0:1, :]).astype(BF16)
    q1 = (q1_ref[0].astype(F32) * up[1:2, :]).astype(BF16)

    def one(q, kc, vc, m, l, a, mask):
        s = lax.dot_general(q, kc, (((1,), (1,)), ((), ())), preferred_element_type=F32)
        if mask is not None:
            s = jnp.where(mask, s, NEG)
        mn = jnp.maximum(m, jnp.max(s, axis=-1, keepdims=True))
        al = jnp.exp2(m - mn)
        p = jnp.exp2(s - mn)
        l = al * l + jnp.sum(p, axis=-1, keepdims=True)
        a = al * a + jnp.dot(p.astype(BF16), vc, preferred_element_type=F32)
        return mn, l, a

    def step(start, size, carry, mask):
        m0, l0, a0, m1, l1, a1 = carry
        vc = v_ref[0, 0, pl.ds(start, size), :]
        m0, l0, a0 = one(q0, k0_ref[0, 0, pl.ds(start, size), :], vc, m0, l0, a0, mask)
        m1, l1, a1 = one(q1, k1_ref[0, 0, pl.ds(start, size), :], vc, m1, l1, a1, mask)
        return m0, l0, a0, m1, l1, a1

    init = (jnp.full((tq, 1), NEG, F32), jnp.zeros((tq, 1), F32), jnp.zeros((tq, LANES), F32)) * 2
    carry = lax.fori_loop(
        0, n_main, lambda c, cr: step(pl.multiple_of(c * tk, tk), tk, cr, None), init)
    tail_mask = lax.broadcasted_iota(I32, (tq, SEQ_PAD), 1) < N_META
    m0, l0, a0, m1, l1, a1 = step(n_main * tk, SEQ_PAD, carry, tail_mask)

    o = up[2:3, :] * (a0 / l0) + up[3:4, :] * (a1 / l1)
    r = lax.rsqrt(jnp.mean(o * o, axis=-1, keepdims=True) + EPS)
    use = up[5:6, :]
    o_ref[...] = (o * (use * r + (1.0 - use)) * up[4:5, :]).astype(BF16)


def _attention(tbl, q, k, v, up, *, batch, l_pad, tq, tk):
    n_pad = q.shape[1]
    nq = l_pad // tq
    n_main = (l_pad - SEQ_PAD) // tk
    k4 = k.reshape(NK, batch, l_pad, LANES)
    v4 = v.reshape(NV, batch, l_pad, LANES)
    qspec = lambda row: pl.BlockSpec((1, tq, LANES), lambda b, u, i, t: (t[row, u], b * nq + i, 0))
    kspec = lambda row: pl.BlockSpec((1, 1, l_pad, LANES), lambda b, u, i, t: (t[row, u], b, 0, 0))
    grid_spec = pltpu.PrefetchScalarGridSpec(
        num_scalar_prefetch=1,
        grid=(batch, NU, nq),
        in_specs=[qspec(0), qspec(1), kspec(2), kspec(3), kspec(4),
                  pl.BlockSpec((1, 8, LANES), lambda b, u, i, t: (u, 0, 0))],
        out_specs=pl.BlockSpec((tq, LANES), lambda b, u, i, t: (b * nq + i, u)),
    )
    return pl.pallas_call(
        functools.partial(_attn_kernel, n_main=n_main, tk=tk, tq=tq),
        grid_spec=grid_spec,
        out_shape=jax.ShapeDtypeStruct((n_pad, NU * LANES), BF16),
        compiler_params=_cparams(("parallel", "parallel", "arbitrary")),
        name="attention",
    )(tbl, q, q, k4, k4, v4, up)


def _out_proj_kernel(mix_ref, h_ref, wout_ref, ln_ref, wrh_ref, wrl_ref, h1_ref, xt_ref, aff_ref, *, tm, l_real,
                     l_pad):
    h1 = h_ref[...] + jnp.dot(mix_ref[...], wout_ref[...], preferred_element_type=F32)
    h1_ref[...] = h1
    ms = jnp.mean(h1 * h1, axis=-1, keepdims=True)
    xt = h1 * lax.rsqrt(ms + EPS) * ln_ref[...]
    xt_ref[...] = xt
    xh = xt.astype(BF16)
    xl = (xt - xh.astype(F32)).astype(BF16)
    lg = (jnp.dot(xh, wrh_ref[...], preferred_element_type=F32)
          + jnp.dot(xl, wrh_ref[...], preferred_element_type=F32)
          + jnp.dot(xh, wrl_ref[...], preferred_element_type=F32))
    lt = lg.T[:N_EXPERTS, :]
    mx = jnp.max(lt, axis=0, keepdims=True)
    ex = jnp.exp(lt - mx)
    aff = ex / jnp.sum(ex, axis=0, keepdims=True)
    row0 = (pl.program_id(0) % (l_pad // tm)) * tm
    row = row0 + lax.broadcasted_iota(I32, (N_EXPERTS, tm), 1)
    aff_ref[...] = jnp.where(row < l_real, aff, -1.0)


def _out_proj(mix, h, wout, ln, wrh, wrl, *, tm, l_real, l_pad):
    n_pad = h.shape[0]
    full = lambda shape: pl.BlockSpec(shape, lambda i: (0,) * len(shape))
    return pl.pallas_call(
        functools.partial(_out_proj_kernel, tm=tm, l_real=l_real, l_pad=l_pad),
        grid=(n_pad // tm,),
        in_specs=[
            pl.BlockSpec((tm, D_MODEL), lambda i: (i, 0)),
            pl.BlockSpec((tm, D_MODEL), lambda i: (i, 0)),
            full((D_MODEL, D_MODEL)), full((1, D_MODEL)), full((D_MODEL, LANES)), full((D_MODEL, LANES)),
        ],
        out_specs=[
            pl.BlockSpec((tm, D_MODEL), lambda i: (i, 0)),
            pl.BlockSpec((tm, D_MODEL), lambda i: (i, 0)),
            pl.BlockSpec((N_EXPERTS, tm), lambda i: (0, i)),
        ],
        out_shape=[
            jax.ShapeDtypeStruct((n_pad, D_MODEL), F32),
            jax.ShapeDtypeStruct((n_pad, D_MODEL), F32),
            jax.ShapeDtypeStruct((N_EXPERTS, n_pad), F32),
        ],
        compiler_params=_cparams(("parallel",)),
        name="out_proj",
    )(mix, h, wout, ln, wrh, wrl)


def _select_kernel(aff_ref, pos_ref, off_ref, idx_ref, *, cap, nb, n_chunks, n_tok):
    li = lax.broadcasted_iota(I32, (LANES, LANES), 0)
    lj = lax.broadcasted_iota(I32, (LANES, LANES), 1)
    upper_incl = (li <= lj).astype(BF16)
    ones_l = jnp.ones((LANES, LANES), BF16)
    bi = lax.broadcasted_iota(I32, (nb, nb), 0)
    bj = lax.broadcasted_iota(I32, (nb, nb), 1)
    lower_strict = (bj < bi).astype(BF16)
    upper_strict = (bi < bj).astype(BF16)
    ones8 = jnp.ones((8, LANES), BF16)
    blk_lane = lax.broadcasted_iota(I32, (LANES, nb), 1)
    slot_sub = lax.broadcasted_iota(I32, (LANES, nb), 0)
    slot_sub_l = lax.broadcasted_iota(I32, (LANES, LANES), 0)

    def prefix(mask_bf):
        pre = jnp.dot(mask_bf, upper_incl, preferred_element_type=F32)
        tot = jnp.dot(mask_bf, ones_l, preferred_element_type=F32)
        off = jnp.dot(lower_strict, tot.astype(BF16), preferred_element_type=F32)
        return pre, tot, off

    def per_expert(e, carry):
        a = aff_ref[e]
        bits = pltpu.bitcast(a, I32)

        def search(i, t):
            cand = t | lax.shift_left(jnp.int32(1), 30 - i)
            cnt = jnp.sum((bits >= cand).astype(I32), axis=(0, 1), keepdims=True)
            return jnp.where(cnt >= cap, cand, t)

        thr = lax.fori_loop(0, 31, search, jnp.zeros((1, 1), I32))
        gt = bits > thr
        eq = bits == thr
        need = (cap - jnp.sum(gt.astype(I32), axis=(0, 1), keepdims=True)).astype(F32)
        pre_e, _, off_e = prefix(jnp.where(eq, 1.0, 0.0).astype(BF16))
        sel = gt | (eq & (pre_e + off_e <= need))
        sel_bf = jnp.where(sel, 1.0, 0.0).astype(BF16)
        pre, tot, off = prefix(sel_bf)
        pos_ref[e] = jnp.where(sel, (pre + off).astype(I32) - 1, -1)
        off_ref[e] = off.astype(I32)

        tot_row = lax.dot_general(ones8, sel_bf, (((1,), (1,)), ((), ())), preferred_element_type=F32)
        off_row = jnp.dot(tot_row.astype(BF16), upper_strict, preferred_element_type=F32)[0:1, :]
        nxt_row = off_row + tot_row[0:1, :]
        pre_bf = pre.astype(BF16)

        def per_chunk(c, carry2):
            r = slot_sub + c * LANES
            nfull = jnp.sum((nxt_row <= r.astype(F32)).astype(I32), axis=1, keepdims=True)
            hot = blk_lane == nfull
            rows = jnp.dot(jnp.where(hot, 1.0, 0.0).astype(BF16), pre_bf, preferred_element_type=F32)
            base = jnp.sum(jnp.where(hot, off_row, 0.0), axis=1, keepdims=True)
            rem = (slot_sub_l[:, 0:1] + c * LANES).astype(F32) - base
            part = jnp.sum((rows <= rem).astype(I32), axis=1, keepdims=True)
            tok = jnp.minimum(nfull * LANES + part, n_tok - 1)
            idx_ref[e, c] = jnp.broadcast_to(tok, (LANES, LANES))
            return carry2

        lax.fori_loop(0, n_chunks, per_chunk, 0)
        return carry

    lax.fori_loop(0, N_EXPERTS, per_expert, 0)


def _select(aff3, *, cap, cap_pad, n_tok):
    nb = aff3.shape[1]
    n_chunks = cap_pad // LANES
    return pl.pallas_call(
        functools.partial(_select_kernel, cap=cap, nb=nb, n_chunks=n_chunks, n_tok=n_tok),
        out_shape=[
            jax.ShapeDtypeStruct((N_EXPERTS, nb, LANES), I32),
            jax.ShapeDtypeStruct((N_EXPERTS, nb, LANES), I32),
            jax.ShapeDtypeStruct((N_EXPERTS, n_chunks, LANES, LANES), I32),
        ],
        compiler_params=pltpu.CompilerParams(vmem_limit_bytes=VMEM_LIMIT),
        name="select",
    )(aff3)


def _gather_kernel(idx_ref, x_hbm, o_ref, sem, *, rows):
    def row_copy(i):
        return pltpu.make_async_copy(x_hbm.at[pl.ds(idx_ref[0, 0, i], 1), :], o_ref.at[0, pl.ds(i, 1), :], sem)

    def start(i, c):
        row_copy(i).start()
        return c

    def wait(i, c):
        row_copy(i).wait()
        return c

    lax.fori_loop(0, rows, start, 0)
    lax.fori_loop(0, rows, wait, 0)


def _gather(idx, xt, *, rows):
    e, cap_pad = idx.shape
    n_t = cap_pad // rows
    return pl.pallas_call(
        functools.partial(_gather_kernel, rows=rows),
        grid=(e, n_t),
        in_specs=[
            pl.BlockSpec((1, 1, rows), lambda i, j: (i * n_t + j, 0, 0), memory_space=pltpu.SMEM),
            pl.BlockSpec(memory_space=pl.ANY),
        ],
        out_specs=pl.BlockSpec((1, rows, D_MODEL), lambda i, j: (i, j, 0)),
        out_shape=jax.ShapeDtypeStruct((e, cap_pad, D_MODEL), F32),
        scratch_shapes=[pltpu.SemaphoreType.DMA(())],
        compiler_params=_cparams(("parallel", "arbitrary")),
        name="gather",
    )(idx.reshape(e * n_t, 1, rows), xt)


def _ffn_kernel(x_ref, wg_ref, wu_ref, wd_ref, o_ref, xb_ref, acc_ref):
    f = pl.program_id(2)

    @pl.when(f == 0)
    def _():
        xb_ref[...] = x_ref[0].astype(BF16)
        acc_ref[...] = jnp.zeros_like(acc_ref)

    x = xb_ref[...]
    g = jnp.dot(x, wg_ref[0].astype(BF16), preferred_element_type=F32)
    u = jnp.dot(x, wu_ref[0].astype(BF16), preferred_element_type=F32)
    hid = (g / (1.0 + jnp.exp(-g)) * u).astype(BF16)
    acc_ref[...] += jnp.dot(hid, wd_ref[0].astype(BF16), preferred_element_type=F32)

    @pl.when(f == pl.num_programs(2) - 1)
    def _():
        o_ref[0] = acc_ref[...].astype(BF16)


def _ffn(xe, w_gate, w_up, w_down, *, rt):
    e, cap_pad, _ = xe.shape
    return pl.pallas_call(
        _ffn_kernel,
        grid=(e, cap_pad // rt, EXPERT_FF // FF_CHUNK),
        in_specs=[
            pl.BlockSpec((1, rt, D_MODEL), lambda i, r, f: (i, r, 0)),
            pl.BlockSpec((1, D_MODEL, FF_CHUNK), lambda i, r, f: (i, 0, f)),
            pl.BlockSpec((1, D_MODEL, FF_CHUNK), lambda i, r, f: (i, 0, f)),
            pl.BlockSpec((1, FF_CHUNK, D_MODEL), lambda i, r, f: (i, f, 0)),
        ],
        out_specs=pl.BlockSpec((1, rt, D_MODEL), lambda i, r, f: (i, r, 0)),
        out_shape=jax.ShapeDtypeStruct((e, cap_pad, D_MODEL), BF16),
        scratch_shapes=[pltpu.VMEM((rt, D_MODEL), BF16), pltpu.VMEM((rt, D_MODEL), F32)],
        compiler_params=_cparams(("parallel", "parallel", "arbitrary")),
        name="ffn",
    )(xe, w_gate, w_up, w_down)


def _combine_kernel(win_ref, pos_ref, aff_ref, y0_ref, y1_ref, h_ref, o_ref):
    t = pl.program_id(0)
    e = pl.program_id(1)

    @pl.when(e == 0)
    def _():
        o_ref[...] = h_ref[...]

    pos = pos_ref[pl.ds(e, 1), :]
    gate = aff_ref[pl.ds(e, 1), :]
    base = win_ref[e, t] * COMBINE_TILE
    slot = lax.broadcasted_iota(I32, (COMBINE_TILE, COMBINE_TILE), 0)
    s0 = jnp.where(pos - base == slot, gate, 0.0).astype(BF16)
    s1 = jnp.where(pos - base - COMBINE_TILE == slot, gate, 0.0).astype(BF16)
    tn = (((0,), (0,)), ((), ()))
    o_ref[...] += (lax.dot_general(s0, y0_ref[0], tn, preferred_element_type=F32)
                   + lax.dot_general(s1, y1_ref[0], tn, preferred_element_type=F32))


def _combine(win, pos, aff, ye, h1):
    n_pad = h1.shape[0]
    e, cap_pad, _ = ye.shape
    last = cap_pad // COMBINE_TILE - 1
    T = COMBINE_TILE
    grid_spec = pltpu.PrefetchScalarGridSpec(
        num_scalar_prefetch=1,
        grid=(n_pad // T, e),
        in_specs=[
            pl.BlockSpec((e, T), lambda t, i, w: (0, t)),
            pl.BlockSpec((e, T), lambda t, i, w: (0, t)),
            pl.BlockSpec((1, T, D_MODEL), lambda t, i, w: (i, w[i, t], 0)),
            pl.BlockSpec((1, T, D_MODEL), lambda t, i, w: (i, jnp.minimum(w[i, t] + 1, last), 0)),
            pl.BlockSpec((T, D_MODEL), lambda t, i, w: (t, 0)),
        ],
        out_specs=pl.BlockSpec((T, D_MODEL), lambda t, i, w: (t, 0)),
    )
    return pl.pallas_call(
        _combine_kernel,
        grid_spec=grid_spec,
        out_shape=jax.ShapeDtypeStruct((n_pad, D_MODEL), F32),
        compiler_params=_cparams(("parallel", "arbitrary")),
        name="combine",
    )(win, pos, aff, ye, ye, h1)


def _rope_tables(s, l_pad):
    del s
    l = np.arange(l_pad)
    real = l - N_META
    pos = l.astype(np.float64)
    row = np.where(l < N_META, -1, real // GRID_W).astype(np.float64)
    col = np.where(l < N_META, l, real % GRID_W).astype(np.float64)

    def cos_sin(p, half):
        ang = p[:, None] * (ROPE_THETA ** (-np.arange(half, dtype=np.float64) / half))[None, :]
        return jnp.asarray(np.cos(ang), F32), jnp.asarray(np.sin(ang), F32)

    def tables(groups):
        c, s1, s2 = [], [], []
        for g in groups:
            if isinstance(g, int):
                c.append(jnp.ones((l_pad, g), F32))
                s1.append(jnp.zeros((l_pad, g), F32))
                s2.append(jnp.zeros((l_pad, g), F32))
            else:
                co, si = g
                z = jnp.zeros_like(si)
                c += [co, co]
                s1 += [-si, z]
                s2 += [z, si]
        return [jnp.concatenate(t, axis=1) for t in (c, s1, s2)]

    a = cos_sin(pos, 32)
    b = cos_sin(pos, 16)
    rr, cc = cos_sin(row, 16), cos_sin(col, 16)
    return jnp.stack(tables([a, a]) + tables([64, b, 32]) + tables([rr, cc, rr, cc]))


def _block_diag(seg):
    i = np.arange(256)
    return (i[:, None] // seg == i[None, :] // seg).astype(np.float32)


def _prep_layer(layer, w_in, w_out, g_qa, g_ka, lam_q1, lam_k1, lam_q2, lam_k2, g_suba, g_cq, w_uq, g_ckv, w_ukv,
                g_qb, g_kb, g_qc, g_kc, w_router):
    zq = w_in[:, 1952:2208].reshape(D_MODEL, HC_KV, HC // HC_KV, DC).transpose(0, 2, 1, 3).reshape(D_MODEL, 256)
    kr = jnp.pad(w_in[:, 1920:1952], ((0, 0), (0, LANES - DB_ROPE)))
    win = jnp.concatenate([w_in[:, :1920], zq, w_in[:, 2208:2464], kr], axis=1).astype(BF16)
    wuq = jnp.pad(w_uq.reshape(Q_LORA, HB, DB_NOPE + DB_ROPE), ((0, 0), (0, 0), (0, 32))).reshape(Q_LORA, 512)
    ukv = w_ukv.reshape(KV_LORA, HB, DB_NOPE + DB_V)
    wkk = jnp.pad(ukv[:, :, :DB_NOPE], ((0, 0), (0, 0), (0, LANES - DB_NOPE))).reshape(KV_LORA, 512)
    wkv = ukv[:, :, DB_NOPE:].reshape(KV_LORA, HB * DB_V)
    oc = w_out[768:].reshape(HC_KV, HC // HC_KV, DC, D_MODEL).transpose(1, 0, 2, 3).reshape(256, D_MODEL)
    wout = jnp.concatenate([w_out[:768], oc], axis=0).astype(BF16)

    pad96 = lambda g: jnp.pad(g, (0, LANES - DB_NOPE - DB_ROPE))
    gv = jnp.stack([
        jnp.tile(g_qa, 2) * (DA ** -0.5 * LOG2E), jnp.tile(g_ka, 2),
        pad96(g_qb) * ((DB_NOPE + DB_ROPE) ** -0.5 * LOG2E), pad96(g_kb),
        jnp.tile(g_qc, 2) * (DC ** -0.5 * LOG2E), jnp.tile(g_kc, 2),
        g_ckv, jnp.zeros((LANES,), F32)]).astype(F32)

    lam_init = 0.8 - 0.6 * math.exp(-0.3 * layer)
    lam = jnp.exp(jnp.sum(lam_q1 * lam_k1)) - jnp.exp(jnp.sum(lam_q2 * lam_k2)) + lam_init
    lo = (jnp.arange(LANES) < 64).astype(F32)
    hi = 1.0 - lo
    one = jnp.ones((LANES,), F32)
    zero = jnp.zeros((LANES,), F32)
    unit_a = jnp.stack([lo, hi, one, -lam * one, g_suba * (1.0 - lam_init), one, zero, zero])
    unit_b = jnp.stack([one, one, lo, hi, one, zero, zero, zero])
    unit_c = jnp.stack([lo, hi, lo, hi, one, zero, zero, zero])
    up = jnp.stack([unit_a] * 4 + [unit_b] * 2 + [unit_c] * 2).astype(F32)

    wr = jnp.pad(w_router, ((0, 0), (0, LANES - N_EXPERTS)))
    wrh = wr.astype(BF16)
    wrl = (wr - wrh.astype(F32)).astype(BF16)
    return dict(win=win, wuq=wuq.astype(BF16), wkk=wkk.astype(BF16), wkv=wkv.astype(BF16), wout=wout, gv=gv,
                gcq=g_cq.reshape(1, Q_LORA).astype(F32), up=up, wrh=wrh, wrl=wrl)


_UNIT_TABLE = np.array([
    [0, 1, 2, 3, 4, 6, 8, 9],
    [0, 1, 2, 3, 5, 7, 8, 9],
    [0, 1, 2, 3, 4, 6, 8, 8],
    [0, 1, 2, 3, 5, 7, 8, 8],
    [0, 1, 2, 3, 4, 5, 6, 6],
], np.int32)


def _tiles(s):
    l_pad = s + SEQ_PAD
    tm = max(t for t in (384, 640, 128) if l_pad % t == 0)
    tk = max(t for t in (1024, 512, 256, 128) if s % t == 0)
    return tm, tk


def _encode(x, meta_tokens, ln_mix, ln_ffn, layers, w_gate, w_up, w_down):
    b, s, d = x.shape
    l_real = s + N_META
    l_pad = s + SEQ_PAD
    n_pad = b * l_pad
    tm, tk = _tiles(s)
    cap = EC_FACTOR * (b * l_real) // N_EXPERTS
    cap_pad = -(-cap // COMBINE_TILE) * COMBINE_TILE
    n_rt = max(1, round(cap_pad / 1100))
    while cap_pad % n_rt or (cap_pad // n_rt) % 16:
        n_rt += 1
    rt = cap_pad // n_rt
    g_rows = max(r for r in range(8, 513, 8) if cap_pad % r == 0)
    nb_pad = -(-(n_pad // LANES) // LANES) * LANES

    h = jnp.concatenate([jnp.broadcast_to(meta_tokens[None], (b, N_META, d)), x,
                         jnp.zeros((b, SEQ_PAD - N_META, d), x.dtype)], axis=1).reshape(n_pad, d)
    rope = _rope_tables(s, l_pad)
    bd = jnp.asarray(np.stack([_block_diag(64), _block_diag(128), _block_diag(256)]), BF16)
    tbl = jnp.asarray(_UNIT_TABLE)

    for l, p in enumerate(layers):
        q, k, v = _in_proj(h, ln_mix[l].reshape(1, d), p["win"], p["wuq"], p["wkk"], p["wkv"], p["gv"], p["gcq"],
                           rope, bd, tm=tm, l_pad=l_pad)
        mix = _attention(tbl, q, k, v, p["up"], batch=b, l_pad=l_pad, tq=tm, tk=tk)
        h1, xt, aff = _out_proj(mix, h, p["wout"], ln_ffn[l].reshape(1, d), p["wrh"], p["wrl"], tm=tm,
                                l_real=l_real, l_pad=l_pad)
        aff3 = jnp.pad(aff, ((0, 0), (0, nb_pad * LANES - n_pad)), constant_values=-1.0)
        pos3, off3, idxb = _select(aff3.reshape(N_EXPERTS, nb_pad, LANES), cap=cap, cap_pad=cap_pad, n_tok=n_pad)
        idx = idxb[:, :, :, 0].reshape(N_EXPERTS, cap_pad)
        pos = pos3.reshape(N_EXPERTS, nb_pad * LANES)[:, :n_pad]
        win = off3[:, ::COMBINE_TILE // LANES, 0][:, :n_pad // COMBINE_TILE] // COMBINE_TILE
        xe = _gather(idx, xt, rows=g_rows)
        ye = _ffn(xe, w_gate[l], w_up[l], w_down[l], rt=rt)
        h = _combine(win, pos, aff, ye, h1)
    return h.reshape(b, l_pad, d)[:, N_META:N_META + s]


def kernel(x_prompt, x_sample, meta_tokens, ln_mix, w_in, w_out, g_qa, g_ka, lam_q1, lam_k1, lam_q2, lam_k2, g_suba,
           g_cq, w_uq, g_ckv, w_ukv, g_qb, g_kb, g_qc, g_kc, ln_ffn, w_router, w_gate, w_up, w_down):
    depth = w_in.shape[0]
    layers = [
        _prep_layer(l, w_in[l], w_out[l], g_qa[l], g_ka[l], lam_q1[l], lam_k1[l], lam_q2[l], lam_k2[l], g_suba[l],
                    g_cq[l], w_uq[l], g_ckv[l], w_ukv[l], g_qb[l], g_kb[l], g_qc[l], g_kc[l], w_router[l])
        for l in range(depth)
    ]
    y_prompt = _encode(x_prompt, meta_tokens, ln_mix, ln_ffn, layers, w_gate, w_up, w_down)
    y_sample = _encode(x_sample, meta_tokens, ln_mix, ln_ffn, layers, w_gate, w_up, w_down)
    return (y_prompt, y_sample)
```

```python
import functools
import math

import jax
import jax.numpy as jnp
import numpy as np
from jax import lax
from jax.experimental import pallas as pl
from jax.experimental.pallas import tpu as pltpu

F32 = jnp.float32
BF16 = jnp.bfloat16
I32 = jnp.int32

D_MODEL = 1024
N_META = 16
GRID_W = 64
ROPE_THETA = 10000.0
EPS = 1e-6
HA, DA, VA = 4, 64, 128
HB, Q_LORA, KV_LORA, DB_NOPE, DB_ROPE, DB_V = 4, 256, 128, 64, 32, 64
HC, HC_KV, DC = 4, 2, 64
N_EXPERTS = 16
EC_FACTOR = 2
EXPERT_FF = 2816
LOG2E = math.log2(math.e)

LANES = 128
SEQ_PAD = 128
ZW = 2560
NQ, NK, NV, NU = 10, 9, 7, 8
FF_CHUNK = 256
MAX_KEY_CHUNK = 1792
COMBINE_WIN = 128
ROW_ALIGN = 256
NEG = -1e30
SAFE_LOGIT = 60.0
VMEM_LIMIT = 56 * 1024 * 1024


def _cparams(sem):
    return pltpu.CompilerParams(dimension_semantics=sem, vmem_limit_bytes=VMEM_LIMIT)


def _seg_rms(x, bd, inv_n):
    ss = jnp.dot((x * x).astype(BF16), bd, preferred_element_type=F32)
    return x * lax.rsqrt(ss * inv_n + EPS)


def _in_proj_kernel(h_ref, ln_ref, win_ref, wuq_ref, wkk_ref, wkv_ref, gv_ref, gcq_ref, rope_ref, bd_ref,
                    q_ref, k_ref, v_ref):
    h = h_ref[...]
    ms = jnp.mean(h * h, axis=-1, keepdims=True)
    u = (h * lax.rsqrt(ms + EPS) * ln_ref[...]).astype(BF16)
    z = jnp.dot(u, win_ref[...], preferred_element_type=F32)
    bd64, bd128, ones256 = bd_ref[0], bd_ref[1], bd_ref[2]

    def rope(x, t, sh):
        return (x * rope_ref[3 * t] + pltpu.roll(x, LANES - sh, 1) * rope_ref[3 * t + 1]
                + pltpu.roll(x, sh, 1) * rope_ref[3 * t + 2])

    def blk(x, j):
        return x[:, j * LANES:(j + 1) * LANES]

    for s in range(2):
        xq = _seg_rms(z[:, s * 256:(s + 1) * 256], bd64, 1.0 / DA)
        xk = _seg_rms(z[:, 512 + s * 256:512 + (s + 1) * 256], bd64, 1.0 / DA)
        for j in range(2):
            q_ref[2 * s + j] = rope(blk(xq, j) * gv_ref[0:1, :], 0, 32).astype(BF16)
            k_ref[2 * s + j] = rope(blk(xk, j) * gv_ref[1:2, :], 0, 32).astype(BF16)
    for j in range(4):
        v_ref[j] = blk(z, 8 + j).astype(BF16)

    cq = (_seg_rms(z[:, 1536:1792], ones256, 1.0 / Q_LORA) * gcq_ref[...]).astype(BF16)
    qb = jnp.dot(cq, wuq_ref[...], preferred_element_type=F32)
    ckv = (_seg_rms(z[:, 1792:1920], bd128[:LANES, :LANES], 1.0 / KV_LORA) * gv_ref[6:7, :]).astype(BF16)
    kk = jnp.dot(ckv, wkk_ref[...], preferred_element_type=F32)
    vb = jnp.dot(ckv, wkv_ref[...], preferred_element_type=F32)
    kr = pltpu.roll(z[:, 2432:2560], 64, 1)
    for s in range(2):
        xq = _seg_rms(qb[:, s * 256:(s + 1) * 256], bd128, 1.0 / (DB_NOPE + DB_ROPE))
        kraw = kk[:, s * 256:(s + 1) * 256] + jnp.concatenate([kr, kr], axis=1)
        xk = _seg_rms(kraw, bd128, 1.0 / (DB_NOPE + DB_ROPE))
        for j in range(2):
            q_ref[4 + 2 * s + j] = rope(blk(xq, j) * gv_ref[2:3, :], 1, 16).astype(BF16)
            k_ref[4 + 2 * s + j] = rope(blk(xk, j) * gv_ref[3:4, :], 1, 16).astype(BF16)
        v_ref[4 + s] = vb[:, s * LANES:(s + 1) * LANES].astype(BF16)

    xq = _seg_rms(z[:, 1920:2176], bd64, 1.0 / DC)
    for j in range(2):
        q_ref[8 + j] = rope(blk(xq, j) * gv_ref[4:5, :], 2, 16).astype(BF16)
    xk = _seg_rms(z[:, 2176:2304], bd64[:LANES, :LANES], 1.0 / DC)
    k_ref[8] = rope(xk * gv_ref[5:6, :], 2, 16).astype(BF16)
    v_ref[6] = z[:, 2304:2432].astype(BF16)


def _in_proj(h, ln, win, wuq, wkk, wkv, gv, gcq, rope, bd, *, tm, l_pad):
    n_pad = h.shape[0]
    tiles_per_seq = l_pad // tm
    full = lambda shape: pl.BlockSpec(shape, lambda i: (0,) * len(shape))
    return pl.pallas_call(
        _in_proj_kernel,
        grid=(n_pad // tm,),
        in_specs=[
            pl.BlockSpec((tm, D_MODEL), lambda i: (i, 0)),
            full((1, D_MODEL)), full((D_MODEL, ZW)), full((Q_LORA, 512)), full((KV_LORA, 512)),
            full((KV_LORA, 256)), full((8, LANES)), full((1, Q_LORA)),
            pl.BlockSpec((9, tm, LANES), lambda i: (0, i % tiles_per_seq, 0)),
            full((3, 256, 256)),
        ],
        out_specs=[
            pl.BlockSpec((NQ, tm, LANES), lambda i: (0, i, 0)),
            pl.BlockSpec((NK, tm, LANES), lambda i: (0, i, 0)),
            pl.BlockSpec((NV, tm, LANES), lambda i: (0, i, 0)),
        ],
        out_shape=[
            jax.ShapeDtypeStruct((NQ, n_pad, LANES), BF16),
            jax.ShapeDtypeStruct((NK, n_pad, LANES), BF16),
            jax.ShapeDtypeStruct((NV, n_pad, LANES), BF16),
        ],
        compiler_params=_cparams(("parallel",)),
        name="in_proj",
    )(h, ln, win, wuq, wkk, wkv, gv, gcq, rope, bd)


def _attn_kernel(tbl_ref, q0_ref, q1_ref, k0_ref, k1_ref, v_ref, up_ref, o_ref, *, n_chunks, tk, tq, l_real):
    up = up_ref[0]
    q0 = (q0_ref[0].astype(F32) * up[0:1, :]).astype(BF16)
    q1 = (q1_ref[0].astype(F32) * up[1:2, :]).astype(BF16)
    nt = (((1,), (1,)), ((), ()))
    last_mask = lax.broadcasted_iota(I32, (tq, tk), 1) < l_real - (n_chunks - 1) * tk

    def finish(a0, l0, a1, l1):
        o = up[2:3, :] * (a0 / l0) + up[3:4, :] * (a1 / l1)
        r = lax.rsqrt(jnp.mean(o * o, axis=-1, keepdims=True) + EPS)
        use = up[5:6, :]
        o_ref[...] = (o * (use * r + (1.0 - use)) * up[4:5, :]).astype(BF16)

    def chunks(step, init, unroll):
        if unroll:
            carry = init
            for c in range(n_chunks - 1):
                carry = step(c * tk, tk, carry, None)
        else:
            carry = lax.fori_loop(0, n_chunks - 1, lambda c, cr: step(pl.multiple_of(c * tk, tk), tk, cr, None),
                                  init)
        return step((n_chunks - 1) * tk, tk, carry, last_mask)

    @pl.when(tbl_ref[5, pl.program_id(1)] != 0)
    def _():
        def one(q, kc, vc, l, a, mask):
            s = lax.dot_general(q, kc, nt, preferred_element_type=F32)
            if mask is not None:
                s = jnp.where(mask, s, NEG)
            p = jnp.exp2(s)
            for j in range(p.shape[1] // LANES):
                l = l + p[:, j * LANES:(j + 1) * LANES]
            return l, a + jnp.dot(p.astype(BF16), vc, preferred_element_type=F32)

        def step(start, size, carry, mask):
            l0, a0, l1, a1 = carry
            vc = v_ref[0, 0, pl.ds(start, size), :]
            l0, a0 = one(q0, k0_ref[0, 0, pl.ds(start, size), :], vc, l0, a0, mask)
            l1, a1 = one(q1, k1_ref[0, 0, pl.ds(start, size), :], vc, l1, a1, mask)
            return l0, a0, l1, a1

        l0, a0, l1, a1 = chunks(step, (jnp.zeros((tq, LANES), F32),) * 4, True)
        finish(a0, jnp.sum(l0, axis=-1, keepdims=True), a1, jnp.sum(l1, axis=-1, keepdims=True))

    @pl.when(tbl_ref[5, pl.program_id(1)] == 0)
    def _():
        def one(q, kc, vc, m, l, a, mask):
            s = lax.dot_general(q, kc, nt, preferred_element_type=F32)
            if mask is not None:
                s = jnp.where(mask, s, NEG)
            mn = jnp.maximum(m, jnp.max(s, axis=-1, keepdims=True))
            al = jnp.exp2(m - mn)
            p = jnp.exp2(s - mn)
            l = al * l + jnp.sum(p, axis=-1, keepdims=True)
            a = al * a + jnp.dot(p.astype(BF16), vc, preferred_element_type=F32)
            return mn, l, a

        def step(start, size, carry, mask):
            m0, l0, a0, m1, l1, a1 = carry
            vc = v_ref[0, 0, pl.ds(start, size), :]
            m0, l0, a0 = one(q0, k0_ref[0, 0, pl.ds(start, size), :], vc, m0, l0, a0, mask)
            m1, l1, a1 = one(q1, k1_ref[0, 0, pl.ds(start, size), :], vc, m1, l1, a1, mask)
            return m0, l0, a0, m1, l1, a1

        init = (jnp.full((tq, 1), NEG, F32), jnp.zeros((tq, 1), F32), jnp.zeros((tq, LANES), F32)) * 2
        _, l0, a0, _, l1, a1 = chunks(step, init, False)
        finish(a0, l0, a1, l1)


def _attention(tbl, q, k, v, up, *, batch, l_pad, tq, n_chunks):
    n_pad = q.shape[1]
    nq = l_pad // tq
    tk = l_pad // n_chunks
    k4 = k.reshape(NK, batch, l_pad, LANES)
    v4 = v.reshape(NV, batch, l_pad, LANES)
    qspec = lambda row: pl.BlockSpec((1, tq, LANES), lambda b, u, i, t: (t[row, u], b * nq + i, 0))
    kspec = lambda row: pl.BlockSpec((1, 1, l_pad, LANES), lambda b, u, i, t: (t[row, u], b, 0, 0))
    grid_spec = pltpu.PrefetchScalarGridSpec(
        num_scalar_prefetch=1,
        grid=(batch, NU, nq),
        in_specs=[qspec(0), qspec(1), kspec(2), kspec(3), kspec(4),
                  pl.BlockSpec((1, 8, LANES), lambda b, u, i, t: (u, 0, 0))],
        out_specs=pl.BlockSpec((tq, LANES), lambda b, u, i, t: (b * nq + i, u)),
    )
    return pl.pallas_call(
        functools.partial(_attn_kernel, n_chunks=n_chunks, tk=tk, tq=tq, l_real=l_pad - SEQ_PAD + N_META),
        grid_spec=grid_spec,
        out_shape=jax.ShapeDtypeStruct((n_pad, NU * LANES), BF16),
        compiler_params=_cparams(("parallel", "parallel", "arbitrary")),
        name="attention",
    )(tbl, q, q, k4, k4, v4, up)


def _out_proj_kernel(mix_ref, h_ref, wout_ref, ln_ref, wrh_ref, wrl_ref, h1_ref, xt_ref, aff_ref, *, tm, l_real,
                     l_pad):
    h1 = h_ref[...] + jnp.dot(mix_ref[...], wout_ref[...], preferred_element_type=F32)
    h1_ref[...] = h1
    ms = jnp.mean(h1 * h1, axis=-1, keepdims=True)
    xt = h1 * lax.rsqrt(ms + EPS) * ln_ref[...]
    for j in range(D_MODEL // LANES):
        xt_ref[:, j, :] = xt[:, j * LANES:(j + 1) * LANES]
    xh = xt.astype(BF16)
    xl = (xt - xh.astype(F32)).astype(BF16)
    lg = (jnp.dot(xh, wrh_ref[...], preferred_element_type=F32)
          + jnp.dot(xl, wrh_ref[...], preferred_element_type=F32)
          + jnp.dot(xh, wrl_ref[...], preferred_element_type=F32))
    lt = lg.T[:N_EXPERTS, :]
    mx = jnp.max(lt, axis=0, keepdims=True)
    ex = jnp.exp(lt - mx)
    aff = ex / jnp.sum(ex, axis=0, keepdims=True)
    row0 = (pl.program_id(0) % (l_pad // tm)) * tm
    row = row0 + lax.broadcasted_iota(I32, (N_EXPERTS, tm), 1)
    aff_ref[...] = jnp.where(row < l_real, aff, -1.0)


def _out_proj(mix, h, wout, ln, wrh, wrl, *, tm, l_real, l_pad):
    n_pad = h.shape[0]
    full = lambda shape: pl.BlockSpec(shape, lambda i: (0,) * len(shape))
    return pl.pallas_call(
        functools.partial(_out_proj_kernel, tm=tm, l_real=l_real, l_pad=l_pad),
        grid=(n_pad // tm,),
        in_specs=[
            pl.BlockSpec((tm, D_MODEL), lambda i: (i, 0)),
            pl.BlockSpec((tm, D_MODEL), lambda i: (i, 0)),
            full((D_MODEL, D_MODEL)), full((1, D_MODEL)), full((D_MODEL, LANES)), full((D_MODEL, LANES)),
        ],
        out_specs=[
            pl.BlockSpec((tm, D_MODEL), lambda i: (i, 0)),
            pl.BlockSpec((tm, D_MODEL // LANES, LANES), lambda i: (i, 0, 0)),
            pl.BlockSpec((N_EXPERTS, tm), lambda i: (0, i)),
        ],
        out_shape=[
            jax.ShapeDtypeStruct((n_pad, D_MODEL), F32),
            jax.ShapeDtypeStruct((n_pad, D_MODEL // LANES, LANES), F32),
            jax.ShapeDtypeStruct((N_EXPERTS, n_pad), F32),
        ],
        compiler_params=_cparams(("parallel",)),
        name="out_proj",
    )(mix, h, wout, ln, wrh, wrl)


def _select_kernel(aff_ref, pos_ref, off_ref, idx_ref, *, cap, nb, n_chunks, n_tok):
    li = lax.broadcasted_iota(I32, (LANES, LANES), 0)
    lj = lax.broadcasted_iota(I32, (LANES, LANES), 1)
    upper_incl = (li <= lj).astype(BF16)
    ones_l = jnp.ones((LANES, LANES), BF16)
    bi = lax.broadcasted_iota(I32, (nb, nb), 0)
    bj = lax.broadcasted_iota(I32, (nb, nb), 1)
    lower_strict = (bj < bi).astype(BF16)
    upper_strict = (bi < bj).astype(BF16)
    ones8 = jnp.ones((8, LANES), BF16)
    blk_lane = lax.broadcasted_iota(I32, (LANES, nb), 1)
    slot_sub = lax.broadcasted_iota(I32, (LANES, nb), 0)
    slot_sub_l = lax.broadcasted_iota(I32, (LANES, LANES), 0)

    def prefix(mask_bf):
        pre = jnp.dot(mask_bf, upper_incl, preferred_element_type=F32)
        tot = jnp.dot(mask_bf, ones_l, preferred_element_type=F32)
        off = jnp.dot(lower_strict, tot.astype(BF16), preferred_element_type=F32)
        return pre, tot, off

    def per_expert(e, carry):
        a = aff_ref[e]
        bits = pltpu.bitcast(a, I32)

        def search(i, t):
            cand = t | lax.shift_left(jnp.int32(1), 30 - i)
            cnt = jnp.sum((bits >= cand).astype(I32), axis=(0, 1), keepdims=True)
            return jnp.where(cnt >= cap, cand, t)

        thr = lax.fori_loop(0, 31, search, jnp.zeros((1, 1), I32))
        gt = bits > thr
        eq = bits == thr
        need = (cap - jnp.sum(gt.astype(I32), axis=(0, 1), keepdims=True)).astype(F32)
        pre_e, _, off_e = prefix(jnp.where(eq, 1.0, 0.0).astype(BF16))
        sel = gt | (eq & (pre_e + off_e <= need))
        sel_bf = jnp.where(sel, 1.0, 0.0).astype(BF16)
        pre, tot, off = prefix(sel_bf)
        pos_ref[e] = jnp.where(sel, (pre + off).astype(I32) - 1, -1)
        off_ref[e] = off.astype(I32)

        tot_row = lax.dot_general(ones8, sel_bf, (((1,), (1,)), ((), ())), preferred_element_type=F32)
        off_row = jnp.dot(tot_row.astype(BF16), upper_strict, preferred_element_type=F32)[0:1, :]
        nxt_row = off_row + tot_row[0:1, :]
        pre_bf = pre.astype(BF16)

        def per_chunk(c, carry2):
            r = slot_sub + c * LANES
            nfull = jnp.sum((nxt_row <= r.astype(F32)).astype(I32), axis=1, keepdims=True)
            hot = blk_lane == nfull
            rows = jnp.dot(jnp.where(hot, 1.0, 0.0).astype(BF16), pre_bf, preferred_element_type=F32)
            base = jnp.sum(jnp.where(hot, off_row, 0.0), axis=1, keepdims=True)
            rem = (slot_sub_l[:, 0:1] + c * LANES).astype(F32) - base
            part = jnp.sum((rows <= rem).astype(I32), axis=1, keepdims=True)
            tok = jnp.minimum(nfull * LANES + part, n_tok - 1)
            idx_ref[e, c] = jnp.broadcast_to(tok, (LANES, LANES))
            return carry2

        lax.fori_loop(0, n_chunks, per_chunk, 0)
        return carry

    lax.fori_loop(0, N_EXPERTS, per_expert, 0)


def _select(aff3, *, cap, cap_pad, n_tok):
    nb = aff3.shape[1]
    n_chunks = cap_pad // LANES
    return pl.pallas_call(
        functools.partial(_select_kernel, cap=cap, nb=nb, n_chunks=n_chunks, n_tok=n_tok),
        out_shape=[
            jax.ShapeDtypeStruct((N_EXPERTS, nb, LANES), I32),
            jax.ShapeDtypeStruct((N_EXPERTS, nb, LANES), I32),
            jax.ShapeDtypeStruct((N_EXPERTS, n_chunks, LANES, LANES), I32),
        ],
        compiler_params=pltpu.CompilerParams(vmem_limit_bytes=VMEM_LIMIT),
        name="select",
    )(aff3)


def _gather_kernel(idx_ref, x_hbm, o_ref, sem, *, rows):
    def start(i, c):
        pltpu.make_async_copy(x_hbm.at[pl.ds(idx_ref[0, 0, i], 1)], o_ref.at[0, pl.ds(i, 1)], sem).start()
        return c

    lax.fori_loop(0, rows, start, 0, unroll=8)
    pltpu.make_async_copy(x_hbm.at[pl.ds(0, rows)], o_ref.at[0], sem).wait()


def _gather(idx, xt3, *, rows):
    e, cap_pad = idx.shape
    n_t = cap_pad // rows
    return pl.pallas_call(
        functools.partial(_gather_kernel, rows=rows),
        grid=(e, n_t),
        in_specs=[
            pl.BlockSpec((1, 1, rows), lambda i, j: (i * n_t + j, 0, 0), memory_space=pltpu.SMEM),
            pl.BlockSpec(memory_space=pl.ANY),
        ],
        out_specs=pl.BlockSpec((1, rows, 8, LANES), lambda i, j: (i, j, 0, 0)),
        out_shape=jax.ShapeDtypeStruct((e, cap_pad, 8, LANES), F32),
        scratch_shapes=[pltpu.SemaphoreType.DMA(())],
        compiler_params=_cparams(("parallel", "arbitrary")),
        name="gather",
    )(idx.reshape(e * n_t, 1, rows), xt3)


def _ffn_kernel(x_ref, wg_ref, wu_ref, wd_ref, o_ref, xb_ref, acc_ref):
    f = pl.program_id(2)

    @pl.when(f == 0)
    def _():
        for j in range(D_MODEL // LANES):
            xb_ref[:, j * LANES:(j + 1) * LANES] = x_ref[0, :, j, :].astype(BF16)
        acc_ref[...] = jnp.zeros_like(acc_ref)

    x = xb_ref[...]
    g = jnp.dot(x, wg_ref[0].astype(BF16), preferred_element_type=F32)
    u = jnp.dot(x, wu_ref[0].astype(BF16), preferred_element_type=F32)
    hid = (g / (1.0 + jnp.exp(-g)) * u).astype(BF16)
    acc_ref[...] += jnp.dot(hid, wd_ref[0].astype(BF16), preferred_element_type=F32)

    @pl.when(f == pl.num_programs(2) - 1)
    def _():
        o_ref[0] = acc_ref[...].astype(BF16)


def _ffn(xe, w_gate, w_up, w_down, *, rt):
    e, cap_pad = xe.shape[:2]
    return pl.pallas_call(
        _ffn_kernel,
        grid=(e, cap_pad // rt, EXPERT_FF // FF_CHUNK),
        in_specs=[
            pl.BlockSpec((1, rt, D_MODEL // LANES, LANES), lambda i, r, f: (i, r, 0, 0)),
            pl.BlockSpec((1, D_MODEL, FF_CHUNK), lambda i, r, f: (i, 0, f)),
            pl.BlockSpec((1, D_MODEL, FF_CHUNK), lambda i, r, f: (i, 0, f)),
            pl.BlockSpec((1, FF_CHUNK, D_MODEL), lambda i, r, f: (i, f, 0)),
        ],
        out_specs=pl.BlockSpec((1, rt, D_MODEL), lambda i, r, f: (i, r, 0)),
        out_shape=jax.ShapeDtypeStruct((e, cap_pad, D_MODEL), BF16),
        scratch_shapes=[pltpu.VMEM((rt, D_MODEL), BF16), pltpu.VMEM((rt, D_MODEL), F32)],
        compiler_params=_cparams(("parallel", "parallel", "arbitrary")),
        name="ffn",
    )(xe, w_gate, w_up, w_down)


def _combine_kernel(off_ref, pos_ref, aff_ref, ye_hbm, h_ref, o_ref, yw_ref, st_ref, sem, *, tile):
    t = pl.program_id(0)

    @pl.when(t == 0)
    def _():
        yw_ref[...] = jnp.zeros_like(yw_ref)

    o_ref[...] = h_ref[...]

    def window(e, k):
        base = off_ref[e, t]
        end = off_ref[e, t + 1]
        first = lax.shift_left(lax.shift_right_logical(base, 4), 4) + k * COMBINE_WIN
        return first, jnp.logical_and(end > base, end > first)

    n_rounds = jnp.int32(0)
    for e in range(N_EXPERTS):
        first, _ = window(e, 0)
        span = jnp.where(off_ref[e, t + 1] > off_ref[e, t], off_ref[e, t + 1] - first, 0)
        n_rounds = jnp.maximum(n_rounds, lax.shift_right_logical(span + (COMBINE_WIN - 1), 7))

    slot = lax.broadcasted_iota(I32, (COMBINE_WIN, tile), 0)

    def copy(e, first):
        return pltpu.make_async_copy(ye_hbm.at[e, pl.ds(pl.multiple_of(first, 16), COMBINE_WIN), :],
                                     yw_ref.at[pl.ds(e * COMBINE_WIN, COMBINE_WIN), :], sem)

    def one_round(k, c):
        for e in range(N_EXPERTS):
            first, active = window(e, k)
            pl.when(active)(copy(e, first).start)
        for e in range(N_EXPERTS):
            first, active = window(e, k)
            first = jnp.where(active, first, -(1 << 30))
            st_ref[e * COMBINE_WIN:(e + 1) * COMBINE_WIN, :] = jnp.where(
                pos_ref[e:e + 1, :] - first == slot, aff_ref[e:e + 1, :], 0.0).astype(BF16)
        for e in range(N_EXPERTS):
            first, active = window(e, k)
            pl.when(active)(copy(e, first).wait)
        o_ref[...] += lax.dot_general(st_ref[...], yw_ref[...], (((0,), (0,)), ((), ())),
                                      preferred_element_type=F32)
        return c

    lax.fori_loop(0, n_rounds, one_round, 0)


def _combine(off, pos, aff, ye, h1, *, tile):
    n_pad = h1.shape[0]
    e = ye.shape[0]
    grid_spec = pltpu.PrefetchScalarGridSpec(
        num_scalar_prefetch=1,
        grid=(n_pad // tile,),
        in_specs=[
            pl.BlockSpec((e, tile), lambda t, o: (0, t)),
            pl.BlockSpec((e, tile), lambda t, o: (0, t)),
            pl.BlockSpec(memory_space=pl.ANY),
            pl.BlockSpec((tile, D_MODEL), lambda t, o: (t, 0)),
        ],
        out_specs=pl.BlockSpec((tile, D_MODEL), lambda t, o: (t, 0)),
        scratch_shapes=[pltpu.VMEM((e * COMBINE_WIN, D_MODEL), BF16), pltpu.VMEM((e * COMBINE_WIN, tile), BF16),
                        pltpu.SemaphoreType.DMA(())],
    )
    return pl.pallas_call(
        functools.partial(_combine_kernel, tile=tile),
        grid_spec=grid_spec,
        out_shape=jax.ShapeDtypeStruct((n_pad, D_MODEL), F32),
        compiler_params=_cparams(("arbitrary",)),
        name="combine",
    )(off, pos, aff, ye, h1)


def _rope_tables(s, l_pad):
    del s
    l = np.arange(l_pad)
    real = l - N_META
    pos = l.astype(np.float64)
    row = np.where(l < N_META, -1, real // GRID_W).astype(np.float64)
    col = np.where(l < N_META, l, real % GRID_W).astype(np.float64)

    def cos_sin(p, half):
        ang = p[:, None] * (ROPE_THETA ** (-np.arange(half, dtype=np.float64) / half))[None, :]
        return jnp.asarray(np.cos(ang), F32), jnp.asarray(np.sin(ang), F32)

    def tables(groups):
        c, s1, s2 = [], [], []
        for g in groups:
            if isinstance(g, int):
                c.append(jnp.ones((l_pad, g), F32))
                s1.append(jnp.zeros((l_pad, g), F32))
                s2.append(jnp.zeros((l_pad, g), F32))
            else:
                co, si = g
                z = jnp.zeros_like(si)
                c += [co, co]
                s1 += [-si, z]
                s2 += [z, si]
        return [jnp.concatenate(t, axis=1) for t in (c, s1, s2)]

    a = cos_sin(pos, 32)
    b = cos_sin(pos, 16)
    rr, cc = cos_sin(row, 16), cos_sin(col, 16)
    return jnp.stack(tables([a, a]) + tables([64, b, 32]) + tables([rr, cc, rr, cc]))


def _block_diag(seg):
    i = np.arange(256)
    return (i[:, None] // seg == i[None, :] // seg).astype(np.float32)


def _prep_layer(layer, w_in, w_out, g_qa, g_ka, lam_q1, lam_k1, lam_q2, lam_k2, g_suba, g_cq, w_uq, g_ckv, w_ukv,
                g_qb, g_kb, g_qc, g_kc, w_router):
    zq = w_in[:, 1952:2208].reshape(D_MODEL, HC_KV, HC // HC_KV, DC).transpose(0, 2, 1, 3).reshape(D_MODEL, 256)
    kr = jnp.pad(w_in[:, 1920:1952], ((0, 0), (0, LANES - DB_ROPE)))
    win = jnp.concatenate([w_in[:, :1920], zq, w_in[:, 2208:2464], kr], axis=1).astype(BF16)
    wuq = jnp.pad(w_uq.reshape(Q_LORA, HB, DB_NOPE + DB_ROPE), ((0, 0), (0, 0), (0, 32))).reshape(Q_LORA, 512)
    ukv = w_ukv.reshape(KV_LORA, HB, DB_NOPE + DB_V)
    wkk = jnp.pad(ukv[:, :, :DB_NOPE], ((0, 0), (0, 0), (0, LANES - DB_NOPE))).reshape(KV_LORA, 512)
    wkv = ukv[:, :, DB_NOPE:].reshape(KV_LORA, HB * DB_V)
    oc = w_out[768:].reshape(HC_KV, HC // HC_KV, DC, D_MODEL).transpose(1, 0, 2, 3).reshape(256, D_MODEL)
    wout = jnp.concatenate([w_out[:768], oc], axis=0).astype(BF16)

    pad96 = lambda g: jnp.pad(g, (0, LANES - DB_NOPE - DB_ROPE))
    gv = jnp.stack([
        jnp.tile(g_qa, 2) * (DA ** -0.5 * LOG2E), jnp.tile(g_ka, 2),
        pad96(g_qb) * ((DB_NOPE + DB_ROPE) ** -0.5 * LOG2E), pad96(g_kb),
        jnp.tile(g_qc, 2) * (DC ** -0.5 * LOG2E), jnp.tile(g_kc, 2),
        g_ckv, jnp.zeros((LANES,), F32)]).astype(F32)

    lam_init = 0.8 - 0.6 * math.exp(-0.3 * layer)
    lam = jnp.exp(jnp.sum(lam_q1 * lam_k1)) - jnp.exp(jnp.sum(lam_q2 * lam_k2)) + lam_init
    lo = (jnp.arange(LANES) < 64).astype(F32)
    hi = 1.0 - lo
    one = jnp.ones((LANES,), F32)
    zero = jnp.zeros((LANES,), F32)
    unit_a = jnp.stack([lo, hi, one, -lam * one, g_suba * (1.0 - lam_init), one, zero, zero])
    unit_b = jnp.stack([one, one, lo, hi, one, zero, zero, zero])
    unit_c = jnp.stack([lo, hi, lo, hi, one, zero, zero, zero])
    up = jnp.stack([unit_a] * 4 + [unit_b] * 2 + [unit_c] * 2).astype(F32)

    wr = jnp.pad(w_router, ((0, 0), (0, LANES - N_EXPERTS)))
    wrh = wr.astype(BF16)
    wrl = (wr - wrh.astype(F32)).astype(BF16)
    def bounded(n, gq, gk):
        return n * jnp.max(jnp.abs(gq)) * jnp.max(jnp.abs(gk)) <= SAFE_LOGIT

    flags = jnp.stack([bounded(DA, gv[0], gv[1])] * HA + [bounded(DB_NOPE + DB_ROPE, gv[2], gv[3])] * 2
                      + [bounded(DC, gv[4], gv[5])] * 2).astype(I32)
    tbl = jnp.concatenate([jnp.asarray(_UNIT_TABLE), flags[None, :]], axis=0)
    return dict(win=win, wuq=wuq.astype(BF16), wkk=wkk.astype(BF16), wkv=wkv.astype(BF16), wout=wout, gv=gv,
                gcq=g_cq.reshape(1, Q_LORA).astype(F32), up=up, wrh=wrh, wrl=wrl, tbl=tbl)


_UNIT_TABLE = np.array([
    [0, 1, 2, 3, 4, 6, 8, 9],
    [0, 1, 2, 3, 5, 7, 8, 9],
    [0, 1, 2, 3, 4, 6, 8, 8],
    [0, 1, 2, 3, 5, 7, 8, 8],
    [0, 1, 2, 3, 4, 5, 6, 6],
], np.int32)


def _tiles(s):
    l_pad = s + SEQ_PAD
    tm = max(t for t in (384, 640, 128) if l_pad % t == 0)
    blocks = l_pad // LANES
    n_chunks = min(n for n in range(1, blocks + 1) if blocks % n == 0 and l_pad // n <= MAX_KEY_CHUNK)
    return tm, n_chunks


def _encode(x, meta_tokens, ln_mix, ln_ffn, layers, w_gate, w_up, w_down):
    b, s, d = x.shape
    l_real = s + N_META
    l_pad = s + SEQ_PAD
    n_pad = b * l_pad
    tm, n_chunks = _tiles(s)
    cap = EC_FACTOR * (b * l_real) // N_EXPERTS
    cap_pad = -(-(cap + COMBINE_WIN) // ROW_ALIGN) * ROW_ALIGN
    ctile = max(t for t in (512, 256, 128) if n_pad % t == 0)
    n_rt = max(1, round(cap_pad / 1100))
    while cap_pad % n_rt or (cap_pad // n_rt) % 16:
        n_rt += 1
    rt = cap_pad // n_rt
    g_rows = max(r for r in range(8, 513, 8) if cap_pad % r == 0)
    nb_pad = -(-(n_pad // LANES) // LANES) * LANES

    h = jnp.concatenate([jnp.broadcast_to(meta_tokens[None], (b, N_META, d)), x,
                         jnp.zeros((b, SEQ_PAD - N_META, d), x.dtype)], axis=1).reshape(n_pad, d)
    rope = _rope_tables(s, l_pad)
    bd = jnp.asarray(np.stack([_block_diag(64), _block_diag(128), _block_diag(256)]), BF16)

    for l, p in enumerate(layers):
        q, k, v = _in_proj(h, ln_mix[l].reshape(1, d), p["win"], p["wuq"], p["wkk"], p["wkv"], p["gv"], p["gcq"],
                           rope, bd, tm=tm, l_pad=l_pad)
        mix = _attention(p["tbl"], q, k, v, p["up"], batch=b, l_pad=l_pad, tq=tm, n_chunks=n_chunks)
        h1, xt, aff = _out_proj(mix, h, p["wout"], ln_ffn[l].reshape(1, d), p["wrh"], p["wrl"], tm=tm,
                                l_real=l_real, l_pad=l_pad)
        aff3 = jnp.pad(aff, ((0, 0), (0, nb_pad * LANES - n_pad)), constant_values=-1.0)
        pos3, off3, idxb = _select(aff3.reshape(N_EXPERTS, nb_pad, LANES), cap=cap, cap_pad=cap_pad, n_tok=n_pad)
        idx = idxb[:, :, :, 0].reshape(N_EXPERTS, cap_pad)
        pos = pos3.reshape(N_EXPERTS, nb_pad * LANES)[:, :n_pad]
        off = off3[:, ::ctile // LANES, 0][:, :n_pad // ctile + 1]
        xe = _gather(idx, xt, rows=g_rows)
        ye = _ffn(xe, w_gate[l], w_up[l], w_down[l], rt=rt)
        h = _combine(off, pos, aff, ye, h1, tile=ctile)
    return h.reshape(b, l_pad, d)[:, N_META:N_META + s]


def kernel(x_prompt, x_sample, meta_tokens, ln_mix, w_in, w_out, g_qa, g_ka, lam_q1, lam_k1, lam_q2, lam_k2, g_suba,
           g_cq, w_uq, g_ckv, w_ukv, g_qb, g_kb, g_qc, g_kc, ln_ffn, w_router, w_gate, w_up, w_down):
    depth = w_in.shape[0]
    layers = [
        _prep_layer(l, w_in[l], w_out[l], g_qa[l], g_ka[l], lam_q1[l], lam_k1[l], lam_q2[l], lam_k2[l], g_suba[l],
                    g_cq[l], w_uq[l], g_ckv[l], w_ukv[l], g_qb[l], g_kb[l], g_qc[l], g_kc[l], w_router[l])
        for l in range(depth)
    ]
    y_prompt = _encode(x_prompt, meta_tokens, ln_mix, ln_ffn, layers, w_gate, w_up, w_down)
    y_sample = _encode(x_sample, meta_tokens, ln_mix, ln_ffn, layers, w_gate, w_up, w_down)
    return (y_prompt, y_sample)
```

```python
import functools
import math

import jax
import jax.numpy as jnp
import numpy as np
from jax import lax
from jax.experimental import pallas as pl
from jax.experimental.pallas import tpu as pltpu

F32 = jnp.float32
BF16 = jnp.bfloat16
I32 = jnp.int32

D_MODEL = 1024
N_META = 16
GRID_W = 64
ROPE_THETA = 10000.0
EPS = 1e-6
HA, DA, VA = 4, 64, 128
HB, Q_LORA, KV_LORA, DB_NOPE, DB_ROPE, DB_V = 4, 256, 128, 64, 32, 64
HC, HC_KV, DC = 4, 2, 64
N_EXPERTS = 16
EC_FACTOR = 2
EXPERT_FF = 2816
LOG2E = math.log2(math.e)

LANES = 128
SEQ_PAD = 128
ZW = 2560
NQ, NK, NV, NU = 10, 9, 7, 8
FF_CHUNK = 256
MAX_KEY_CHUNK = 1792
COMBINE_WIN = 128
ROW_ALIGN = 256
NEG = -1e30
SAFE_LOGIT = 60.0
VMEM_LIMIT = 56 * 1024 * 1024


def _cparams(sem):
    return pltpu.CompilerParams(dimension_semantics=sem, vmem_limit_bytes=VMEM_LIMIT)


def _seg_rms(x, bd, inv_n):
    ss = jnp.dot((x * x).astype(BF16), bd, preferred_element_type=F32)
    return x * lax.rsqrt(ss * inv_n + EPS)


def _in_proj_kernel(h_ref, ln_ref, win_ref, wuq_ref, wkk_ref, wkv_ref, gv_ref, gcq_ref, rope_ref, bd_ref,
                    q_ref, k_ref, v_ref):
    h = h_ref[...]
    ms = jnp.mean(h * h, axis=-1, keepdims=True)
    u = (h * lax.rsqrt(ms + EPS) * ln_ref[...]).astype(BF16)
    z = jnp.dot(u, win_ref[...], preferred_element_type=F32)
    bd64, bd128, ones256 = bd_ref[0], bd_ref[1], bd_ref[2]

    def rope(x, t, sh):
        return (x * rope_ref[3 * t] + pltpu.roll(x, LANES - sh, 1) * rope_ref[3 * t + 1]
                + pltpu.roll(x, sh, 1) * rope_ref[3 * t + 2])

    def blk(x, j):
        return x[:, j * LANES:(j + 1) * LANES]

    for s in range(2):
        xq = _seg_rms(z[:, s * 256:(s + 1) * 256], bd64, 1.0 / DA)
        xk = _seg_rms(z[:, 512 + s * 256:512 + (s + 1) * 256], bd64, 1.0 / DA)
        for j in range(2):
            q_ref[2 * s + j] = rope(blk(xq, j) * gv_ref[0:1, :], 0, 32).astype(BF16)
            k_ref[2 * s + j] = rope(blk(xk, j) * gv_ref[1:2, :], 0, 32).astype(BF16)
    for j in range(4):
        v_ref[j] = blk(z, 8 + j).astype(BF16)

    cq = (_seg_rms(z[:, 1536:1792], ones256, 1.0 / Q_LORA) * gcq_ref[...]).astype(BF16)
    qb = jnp.dot(cq, wuq_ref[...], preferred_element_type=F32)
    ckv = (_seg_rms(z[:, 1792:1920], bd128[:LANES, :LANES], 1.0 / KV_LORA) * gv_ref[6:7, :]).astype(BF16)
    kk = jnp.dot(ckv, wkk_ref[...], preferred_element_type=F32)
    vb = jnp.dot(ckv, wkv_ref[...], preferred_element_type=F32)
    kr = pltpu.roll(z[:, 2432:2560], 64, 1)
    for s in range(2):
        xq = _seg_rms(qb[:, s * 256:(s + 1) * 256], bd128, 1.0 / (DB_NOPE + DB_ROPE))
        kraw = kk[:, s * 256:(s + 1) * 256] + jnp.concatenate([kr, kr], axis=1)
        xk = _seg_rms(kraw, bd128, 1.0 / (DB_NOPE + DB_ROPE))
        for j in range(2):
            q_ref[4 + 2 * s + j] = rope(blk(xq, j) * gv_ref[2:3, :], 1, 16).astype(BF16)
            k_ref[4 + 2 * s + j] = rope(blk(xk, j) * gv_ref[3:4, :], 1, 16).astype(BF16)
        v_ref[4 + s] = vb[:, s * LANES:(s + 1) * LANES].astype(BF16)

    xq = _seg_rms(z[:, 1920:2176], bd64, 1.0 / DC)
    for j in range(2):
        q_ref[8 + j] = rope(blk(xq, j) * gv_ref[4:5, :], 2, 16).astype(BF16)
    xk = _seg_rms(z[:, 2176:2304], bd64[:LANES, :LANES], 1.0 / DC)
    k_ref[8] = rope(xk * gv_ref[5:6, :], 2, 16).astype(BF16)
    v_ref[6] = z[:, 2304:2432].astype(BF16)


def _in_proj(h, ln, win, wuq, wkk, wkv, gv, gcq, rope, bd, *, tm, l_pad):
    n_pad = h.shape[0]
    tiles_per_seq = l_pad // tm
    full = lambda shape: pl.BlockSpec(shape, lambda i: (0,) * len(shape))
    return pl.pallas_call(
        _in_proj_kernel,
        grid=(n_pad // tm,),
        in_specs=[
            pl.BlockSpec((tm, D_MODEL), lambda i: (i, 0)),
            full((1, D_MODEL)), full((D_MODEL, ZW)), full((Q_LORA, 512)), full((KV_LORA, 512)),
            full((KV_LORA, 256)), full((8, LANES)), full((1, Q_LORA)),
            pl.BlockSpec((9, tm, LANES), lambda i: (0, i % tiles_per_seq, 0)),
            full((3, 256, 256)),
        ],
        out_specs=[
            pl.BlockSpec((NQ, tm, LANES), lambda i: (0, i, 0)),
            pl.BlockSpec((NK, tm, LANES), lambda i: (0, i, 0)),
            pl.BlockSpec((NV, tm, LANES), lambda i: (0, i, 0)),
        ],
        out_shape=[
            jax.ShapeDtypeStruct((NQ, n_pad, LANES), BF16),
            jax.ShapeDtypeStruct((NK, n_pad, LANES), BF16),
            jax.ShapeDtypeStruct((NV, n_pad, LANES), BF16),
        ],
        compiler_params=_cparams(("parallel",)),
        name="in_proj",
    )(h, ln, win, wuq, wkk, wkv, gv, gcq, rope, bd)


def _attn_kernel(tbl_ref, q0_ref, q1_ref, k0_ref, k1_ref, v_ref, up_ref, o_ref, *, n_chunks, tk, tq, l_real):
    up = up_ref[0]
    q0 = (q0_ref[0].astype(F32) * up[0:1, :]).astype(BF16)
    q1 = (q1_ref[0].astype(F32) * up[1:2, :]).astype(BF16)
    nt = (((1,), (1,)), ((), ()))
    last_mask = lax.broadcasted_iota(I32, (tq, tk), 1) < l_real - (n_chunks - 1) * tk

    def finish(a0, l0, a1, l1):
        o = up[2:3, :] * (a0 / l0) + up[3:4, :] * (a1 / l1)
        r = lax.rsqrt(jnp.mean(o * o, axis=-1, keepdims=True) + EPS)
        use = up[5:6, :]
        o_ref[...] = (o * (use * r + (1.0 - use)) * up[4:5, :]).astype(BF16)

    def chunks(step, init, unroll):
        if unroll:
            carry = init
            for c in range(n_chunks - 1):
                carry = step(c * tk, tk, carry, None)
        else:
            carry = lax.fori_loop(0, n_chunks - 1, lambda c, cr: step(pl.multiple_of(c * tk, tk), tk, cr, None),
                                  init)
        return step((n_chunks - 1) * tk, tk, carry, last_mask)

    @pl.when(tbl_ref[5, pl.program_id(1)] != 0)
    def _():
        def one(q, kc, vc, l, a, mask):
            s = lax.dot_general(q, kc, nt, preferred_element_type=F32)
            if mask is not None:
                s = jnp.where(mask, s, NEG)
            p = jnp.exp2(s)
            for j in range(p.shape[1] // LANES):
                l = l + p[:, j * LANES:(j + 1) * LANES]
            return l, a + jnp.dot(p.astype(BF16), vc, preferred_element_type=F32)

        def step(start, size, carry, mask):
            l0, a0, l1, a1 = carry
            vc = v_ref[0, 0, pl.ds(start, size), :]
            l0, a0 = one(q0, k0_ref[0, 0, pl.ds(start, size), :], vc, l0, a0, mask)
            l1, a1 = one(q1, k1_ref[0, 0, pl.ds(start, size), :], vc, l1, a1, mask)
            return l0, a0, l1, a1

        l0, a0, l1, a1 = chunks(step, (jnp.zeros((tq, LANES), F32),) * 4, True)
        finish(a0, jnp.sum(l0, axis=-1, keepdims=True), a1, jnp.sum(l1, axis=-1, keepdims=True))

    @pl.when(tbl_ref[5, pl.program_id(1)] == 0)
    def _():
        def one(q, kc, vc, m, l, a, mask):
            s = lax.dot_general(q, kc, nt, preferred_element_type=F32)
            if mask is not None:
                s = jnp.where(mask, s, NEG)
            mn = jnp.maximum(m, jnp.max(s, axis=-1, keepdims=True))
            al = jnp.exp2(m - mn)
            p = jnp.exp2(s - mn)
            l = al * l + jnp.sum(p, axis=-1, keepdims=True)
            a = al * a + jnp.dot(p.astype(BF16), vc, preferred_element_type=F32)
            return mn, l, a

        def step(start, size, carry, mask):
            m0, l0, a0, m1, l1, a1 = carry
            vc = v_ref[0, 0, pl.ds(start, size), :]
            m0, l0, a0 = one(q0, k0_ref[0, 0, pl.ds(start, size), :], vc, m0, l0, a0, mask)
            m1, l1, a1 = one(q1, k1_ref[0, 0, pl.ds(start, size), :], vc, m1, l1, a1, mask)
            return m0, l0, a0, m1, l1, a1

        init = (jnp.full((tq, 1), NEG, F32), jnp.zeros((tq, 1), F32), jnp.zeros((tq, LANES), F32)) * 2
        _, l0, a0, _, l1, a1 = chunks(step, init, False)
        finish(a0, l0, a1, l1)


def _attention(tbl, q, k, v, up, *, batch, l_pad, tq, n_chunks):
    n_pad = q.shape[1]
    nq = l_pad // tq
    tk = l_pad // n_chunks
    k4 = k.reshape(NK, batch, l_pad, LANES)
    v4 = v.reshape(NV, batch, l_pad, LANES)
    qspec = lambda row: pl.BlockSpec((1, tq, LANES), lambda b, u, i, t: (t[row, u], b * nq + i, 0))
    kspec = lambda row: pl.BlockSpec((1, 1, l_pad, LANES), lambda b, u, i, t: (t[row, u], b, 0, 0))
    grid_spec = pltpu.PrefetchScalarGridSpec(
        num_scalar_prefetch=1,
        grid=(batch, NU, nq),
        in_specs=[qspec(0), qspec(1), kspec(2), kspec(3), kspec(4),
                  pl.BlockSpec((1, 8, LANES), lambda b, u, i, t: (u, 0, 0))],
        out_specs=pl.BlockSpec((tq, LANES), lambda b, u, i, t: (b * nq + i, u)),
    )
    return pl.pallas_call(
        functools.partial(_attn_kernel, n_chunks=n_chunks, tk=tk, tq=tq, l_real=l_pad - SEQ_PAD + N_META),
        grid_spec=grid_spec,
        out_shape=jax.ShapeDtypeStruct((n_pad, NU * LANES), BF16),
        compiler_params=_cparams(("parallel", "parallel", "arbitrary")),
        name="attention",
    )(tbl, q, q, k4, k4, v4, up)


def _out_proj_kernel(mix_ref, h_ref, wout_ref, ln_ref, wrh_ref, wrl_ref, h1_ref, xt_ref, aff_ref, *, tm, l_real,
                     l_pad):
    h1 = h_ref[...] + jnp.dot(mix_ref[...], wout_ref[...], preferred_element_type=F32)
    h1_ref[...] = h1
    ms = jnp.mean(h1 * h1, axis=-1, keepdims=True)
    xt = h1 * lax.rsqrt(ms + EPS) * ln_ref[...]
    for j in range(D_MODEL // LANES):
        xt_ref[pl.ds(j, tm, stride=D_MODEL // LANES), :] = xt[:, j * LANES:(j + 1) * LANES]
    xh = xt.astype(BF16)
    xl = (xt - xh.astype(F32)).astype(BF16)
    lg = (jnp.dot(xh, wrh_ref[...], preferred_element_type=F32)
          + jnp.dot(xl, wrh_ref[...], preferred_element_type=F32)
          + jnp.dot(xh, wrl_ref[...], preferred_element_type=F32))
    lt = lg.T[:N_EXPERTS, :]
    mx = jnp.max(lt, axis=0, keepdims=True)
    ex = jnp.exp(lt - mx)
    aff = ex / jnp.sum(ex, axis=0, keepdims=True)
    row0 = (pl.program_id(0) % (l_pad // tm)) * tm
    row = row0 + lax.broadcasted_iota(I32, (N_EXPERTS, tm), 1)
    aff_ref[...] = jnp.where(row < l_real, aff, -1.0)


def _out_proj(mix, h, wout, ln, wrh, wrl, *, tm, l_real, l_pad):
    n_pad = h.shape[0]
    full = lambda shape: pl.BlockSpec(shape, lambda i: (0,) * len(shape))
    return pl.pallas_call(
        functools.partial(_out_proj_kernel, tm=tm, l_real=l_real, l_pad=l_pad),
        grid=(n_pad // tm,),
        in_specs=[
            pl.BlockSpec((tm, D_MODEL), lambda i: (i, 0)),
            pl.BlockSpec((tm, D_MODEL), lambda i: (i, 0)),
            full((D_MODEL, D_MODEL)), full((1, D_MODEL)), full((D_MODEL, LANES)), full((D_MODEL, LANES)),
        ],
        out_specs=[
            pl.BlockSpec((tm, D_MODEL), lambda i: (i, 0)),
            pl.BlockSpec((tm * (D_MODEL // LANES), LANES), lambda i: (i, 0)),
            pl.BlockSpec((N_EXPERTS, tm), lambda i: (0, i)),
        ],
        out_shape=[
            jax.ShapeDtypeStruct((n_pad, D_MODEL), F32),
            jax.ShapeDtypeStruct((n_pad * (D_MODEL // LANES), LANES), F32),
            jax.ShapeDtypeStruct((N_EXPERTS, n_pad), F32),
        ],
        compiler_params=_cparams(("parallel",)),
        name="out_proj",
    )(mix, h, wout, ln, wrh, wrl)


def _select_kernel(aff_ref, pos_ref, off_ref, idx_ref, *, cap, nb, n_chunks, n_tok):
    li = lax.broadcasted_iota(I32, (LANES, LANES), 0)
    lj = lax.broadcasted_iota(I32, (LANES, LANES), 1)
    upper_incl = (li <= lj).astype(BF16)
    ones_l = jnp.ones((LANES, LANES), BF16)
    bi = lax.broadcasted_iota(I32, (nb, nb), 0)
    bj = lax.broadcasted_iota(I32, (nb, nb), 1)
    lower_strict = (bj < bi).astype(BF16)
    upper_strict = (bi < bj).astype(BF16)
    ones8 = jnp.ones((8, LANES), BF16)
    blk_lane = lax.broadcasted_iota(I32, (LANES, nb), 1)
    slot_sub = lax.broadcasted_iota(I32, (LANES, nb), 0)
    slot_sub_l = lax.broadcasted_iota(I32, (LANES, LANES), 0)

    def prefix(mask_bf):
        pre = jnp.dot(mask_bf, upper_incl, preferred_element_type=F32)
        tot = jnp.dot(mask_bf, ones_l, preferred_element_type=F32)
        off = jnp.dot(lower_strict, tot.astype(BF16), preferred_element_type=F32)
        return pre, tot, off

    def per_expert(e, carry):
        a = aff_ref[e]
        bits = pltpu.bitcast(a, I32)

        def search(i, t):
            cand = t | lax.shift_left(jnp.int32(1), 30 - i)
            cnt = jnp.sum((bits >= cand).astype(I32), axis=(0, 1), keepdims=True)
            return jnp.where(cnt >= cap, cand, t)

        thr = lax.fori_loop(0, 31, search, jnp.zeros((1, 1), I32))
        gt = bits > thr
        eq = bits == thr
        need = (cap - jnp.sum(gt.astype(I32), axis=(0, 1), keepdims=True)).astype(F32)
        pre_e, _, off_e = prefix(jnp.where(eq, 1.0, 0.0).astype(BF16))
        sel = gt | (eq & (pre_e + off_e <= need))
        sel_bf = jnp.where(sel, 1.0, 0.0).astype(BF16)
        pre, tot, off = prefix(sel_bf)
        pos_ref[e] = jnp.where(sel, (pre + off).astype(I32) - 1, -1)
        off_ref[e] = off.astype(I32)

        tot_row = lax.dot_general(ones8, sel_bf, (((1,), (1,)), ((), ())), preferred_element_type=F32)
        off_row = jnp.dot(tot_row.astype(BF16), upper_strict, preferred_element_type=F32)[0:1, :]
        nxt_row = off_row + tot_row[0:1, :]
        pre_bf = pre.astype(BF16)

        def per_chunk(c, carry2):
            r = slot_sub + c * LANES
            nfull = jnp.sum((nxt_row <= r.astype(F32)).astype(I32), axis=1, keepdims=True)
            hot = blk_lane == nfull
            rows = jnp.dot(jnp.where(hot, 1.0, 0.0).astype(BF16), pre_bf, preferred_element_type=F32)
            base = jnp.sum(jnp.where(hot, off_row, 0.0), axis=1, keepdims=True)
            rem = (slot_sub_l[:, 0:1] + c * LANES).astype(F32) - base
            part = jnp.sum((rows <= rem).astype(I32), axis=1, keepdims=True)
            tok = jnp.minimum(nfull * LANES + part, n_tok - 1)
            idx_ref[e, c] = jnp.broadcast_to(tok, (LANES, LANES)).T[:8, :]
            return carry2

        lax.fori_loop(0, n_chunks, per_chunk, 0)
        return carry

    lax.fori_loop(0, N_EXPERTS, per_expert, 0)


def _select(aff3, *, cap, cap_pad, n_tok):
    nb = aff3.shape[1]
    n_chunks = cap_pad // LANES
    return pl.pallas_call(
        functools.partial(_select_kernel, cap=cap, nb=nb, n_chunks=n_chunks, n_tok=n_tok),
        out_shape=[
            jax.ShapeDtypeStruct((N_EXPERTS, nb, LANES), I32),
            jax.ShapeDtypeStruct((N_EXPERTS, nb, LANES), I32),
            jax.ShapeDtypeStruct((N_EXPERTS, n_chunks, 8, LANES), I32),
        ],
        compiler_params=pltpu.CompilerParams(vmem_limit_bytes=VMEM_LIMIT),
        name="select",
    )(aff3)


def _gather_kernel(idx_ref, nxt_ref, x_hbm, o_hbm, buf, gsem, wsem, *, rows, n_steps):
    s = pl.program_id(0)
    slot = s % 2
    other = 1 - slot

    def start_rows(ids_ref, dst_slot):
        def start(i, c):
            pltpu.make_async_copy(x_hbm.at[pl.ds(ids_ref[0, 0, i], 1)], buf.at[dst_slot, pl.ds(i, 1)],
                                  gsem.at[dst_slot]).start()
            return c
        lax.fori_loop(0, rows, start, 0, unroll=8)

    def writeback(src_slot, step):
        return pltpu.make_async_copy(buf.at[src_slot], o_hbm.at[pl.ds(step * rows, rows)], wsem.at[src_slot])

    @pl.when(s == 0)
    def _():
        start_rows(idx_ref, slot)

    @pl.when(s >= 1)
    def _():
        writeback(other, s - 1).wait()

    @pl.when(s + 1 < n_steps)
    def _():
        start_rows(nxt_ref, other)

    pltpu.make_async_copy(x_hbm.at[pl.ds(0, rows)], buf.at[slot], gsem.at[slot]).wait()
    writeback(slot, s).start()

    @pl.when(s == n_steps - 1)
    def _():
        writeback(slot, s).wait()


def _gather(idx, xt, *, rows):
    e, cap_pad = idx.shape
    n_steps = e * cap_pad // rows
    sub = D_MODEL // LANES
    xt3 = xt.reshape(-1, sub, LANES)
    ids = idx.reshape(n_steps, 1, rows)
    out = pl.pallas_call(
        functools.partial(_gather_kernel, rows=rows, n_steps=n_steps),
        grid=(n_steps,),
        in_specs=[
            pl.BlockSpec((1, 1, rows), lambda s: (s, 0, 0), memory_space=pltpu.SMEM),
            pl.BlockSpec((1, 1, rows), lambda s: (jnp.minimum(s + 1, n_steps - 1), 0, 0), memory_space=pltpu.SMEM),
            pl.BlockSpec(memory_space=pl.ANY),
        ],
        out_specs=pl.BlockSpec(memory_space=pl.ANY),
        out_shape=jax.ShapeDtypeStruct((e * cap_pad, sub, LANES), F32),
        scratch_shapes=[pltpu.VMEM((2, rows, sub, LANES), F32), pltpu.SemaphoreType.DMA((2,)),
                        pltpu.SemaphoreType.DMA((2,))],
        compiler_params=_cparams(("arbitrary",)),
        name="gather",
    )(ids, ids, xt3)
    return out.reshape(e, cap_pad, sub, LANES)


def _ffn_kernel(x_ref, wg_ref, wu_ref, wd_ref, o_ref, xb_ref, acc_ref):
    f = pl.program_id(2)

    @pl.when(f == 0)
    def _():
        rows = xb_ref.shape[0]
        for j in range(D_MODEL // LANES):
            xb_ref[:, j * LANES:(j + 1) * LANES] = x_ref[0, pl.ds(j, rows, stride=D_MODEL // LANES), :].astype(BF16)
        acc_ref[...] = jnp.zeros_like(acc_ref)

    x = xb_ref[...]
    g = jnp.dot(x, wg_ref[0].astype(BF16), preferred_element_type=F32)
    u = jnp.dot(x, wu_ref[0].astype(BF16), preferred_element_type=F32)
    hid = (g / (1.0 + jnp.exp(-g)) * u).astype(BF16)
    acc_ref[...] += jnp.dot(hid, wd_ref[0].astype(BF16), preferred_element_type=F32)

    @pl.when(f == pl.num_programs(2) - 1)
    def _():
        o_ref[0] = acc_ref[...].astype(BF16)


def _ffn(xe, w_gate, w_up, w_down, *, rt):
    e, cap_pad = xe.shape[:2]
    sub = D_MODEL // LANES
    xe = xe.reshape(e, cap_pad * sub, LANES)
    return pl.pallas_call(
        _ffn_kernel,
        grid=(e, cap_pad // rt, EXPERT_FF // FF_CHUNK),
        in_specs=[
            pl.BlockSpec((1, rt * sub, LANES), lambda i, r, f: (i, r, 0)),
            pl.BlockSpec((1, D_MODEL, FF_CHUNK), lambda i, r, f: (i, 0, f)),
            pl.BlockSpec((1, D_MODEL, FF_CHUNK), lambda i, r, f: (i, 0, f)),
            pl.BlockSpec((1, FF_CHUNK, D_MODEL), lambda i, r, f: (i, f, 0)),
        ],
        out_specs=pl.BlockSpec((1, rt, D_MODEL), lambda i, r, f: (i, r, 0)),
        out_shape=jax.ShapeDtypeStruct((e, cap_pad, D_MODEL), BF16),
        scratch_shapes=[pltpu.VMEM((rt, D_MODEL), BF16), pltpu.VMEM((rt, D_MODEL), F32)],
        compiler_params=_cparams(("parallel", "parallel", "arbitrary")),
        name="ffn",
    )(xe, w_gate, w_up, w_down)


def _combine_kernel(off_ref, pos_ref, aff_ref, ye_hbm, h_ref, o_ref, yw_ref, st_ref, sem, *, tile):
    t = pl.program_id(0)
    n_tiles = pl.num_programs(0)
    buf = t % 2

    @pl.when(t == 0)
    def _():
        yw_ref[...] = jnp.zeros_like(yw_ref)

    o_ref[...] = h_ref[...]

    def window(tt, e, k):
        base = off_ref[e, tt]
        end = off_ref[e, tt + 1]
        first = lax.shift_left(lax.shift_right_logical(base, 4), 4) + k * COMBINE_WIN
        return first, jnp.logical_and(end > base, end > first)

    def copy(e, first, b):
        return pltpu.make_async_copy(ye_hbm.at[e, pl.ds(pl.multiple_of(first, 16), COMBINE_WIN), :],
                                     yw_ref.at[b, pl.ds(e * COMBINE_WIN, COMBINE_WIN), :], sem.at[b])

    def start_round(tt, k, b):
        for e in range(N_EXPERTS):
            first, active = window(tt, e, k)
            pl.when(active)(copy(e, first, b).start)

    slot = lax.broadcasted_iota(I32, (COMBINE_WIN, tile), 0)

    def finish_round(k):
        for e in range(N_EXPERTS):
            first, active = window(t, e, k)
            first = jnp.where(active, first, -(1 << 30))
            st_ref[e * COMBINE_WIN:(e + 1) * COMBINE_WIN, :] = jnp.where(
                pos_ref[e:e + 1, :] - first == slot, aff_ref[e:e + 1, :], 0.0).astype(BF16)
        for e in range(N_EXPERTS):
            first, active = window(t, e, k)
            pl.when(active)(copy(e, first, buf).wait)
        o_ref[...] += lax.dot_general(st_ref[...], yw_ref[buf], (((0,), (0,)), ((), ())),
                                      preferred_element_type=F32)

    @pl.when(t == 0)
    def _():
        start_round(t, 0, buf)

    @pl.when(t + 1 < n_tiles)
    def _():
        start_round(t + 1, 0, 1 - buf)

    finish_round(0)

    n_rounds = jnp.int32(0)
    for e in range(N_EXPERTS):
        first, _ = window(t, e, 0)
        span = jnp.where(off_ref[e, t + 1] > off_ref[e, t], off_ref[e, t + 1] - first, 0)
        n_rounds = jnp.maximum(n_rounds, lax.shift_right_logical(span + (COMBINE_WIN - 1), 7))

    def extra_round(k, c):
        start_round(t, k, buf)
        finish_round(k)
        return c

    lax.fori_loop(1, n_rounds, extra_round, 0)


def _combine(off, pos, aff, ye, h1, *, tile):
    n_pad = h1.shape[0]
    e = ye.shape[0]
    grid_spec = pltpu.PrefetchScalarGridSpec(
        num_scalar_prefetch=1,
        grid=(n_pad // tile,),
        in_specs=[
            pl.BlockSpec((e, tile), lambda t, o: (0, t)),
            pl.BlockSpec((e, tile), lambda t, o: (0, t)),
            pl.BlockSpec(memory_space=pl.ANY),
            pl.BlockSpec((tile, D_MODEL), lambda t, o: (t, 0)),
        ],
        out_specs=pl.BlockSpec((tile, D_MODEL), lambda t, o: (t, 0)),
        scratch_shapes=[pltpu.VMEM((2, e * COMBINE_WIN, D_MODEL), BF16), pltpu.VMEM((e * COMBINE_WIN, tile), BF16),
                        pltpu.SemaphoreType.DMA((2,))],
    )
    return pl.pallas_call(
        functools.partial(_combine_kernel, tile=tile),
        grid_spec=grid_spec,
        out_shape=jax.ShapeDtypeStruct((n_pad, D_MODEL), F32),
        compiler_params=_cparams(("arbitrary",)),
        name="combine",
    )(off, pos, aff, ye, h1)


def _rope_tables(s, l_pad):
    del s
    l = np.arange(l_pad)
    real = l - N_META
    pos = l.astype(np.float64)
    row = np.where(l < N_META, -1, real // GRID_W).astype(np.float64)
    col = np.where(l < N_META, l, real % GRID_W).astype(np.float64)

    def cos_sin(p, half):
        ang = p[:, None] * (ROPE_THETA ** (-np.arange(half, dtype=np.float64) / half))[None, :]
        return jnp.asarray(np.cos(ang), F32), jnp.asarray(np.sin(ang), F32)

    def tables(groups):
        c, s1, s2 = [], [], []
        for g in groups:
            if isinstance(g, int):
                c.append(jnp.ones((l_pad, g), F32))
                s1.append(jnp.zeros((l_pad, g), F32))
                s2.append(jnp.zeros((l_pad, g), F32))
            else:
                co, si = g
                z = jnp.zeros_like(si)
                c += [co, co]
                s1 += [-si, z]
                s2 += [z, si]
        return [jnp.concatenate(t, axis=1) for t in (c, s1, s2)]

    a = cos_sin(pos, 32)
    b = cos_sin(pos, 16)
    rr, cc = cos_sin(row, 16), cos_sin(col, 16)
    return jnp.stack(tables([a, a]) + tables([64, b, 32]) + tables([rr, cc, rr, cc]))


def _block_diag(seg):
    i = np.arange(256)
    return (i[:, None] // seg == i[None, :] // seg).astype(np.float32)


def _prep_layer(layer, w_in, w_out, g_qa, g_ka, lam_q1, lam_k1, lam_q2, lam_k2, g_suba, g_cq, w_uq, g_ckv, w_ukv,
                g_qb, g_kb, g_qc, g_kc, w_router):
    zq = w_in[:, 1952:2208].reshape(D_MODEL, HC_KV, HC // HC_KV, DC).transpose(0, 2, 1, 3).reshape(D_MODEL, 256)
    kr = jnp.pad(w_in[:, 1920:1952], ((0, 0), (0, LANES - DB_ROPE)))
    win = jnp.concatenate([w_in[:, :1920], zq, w_in[:, 2208:2464], kr], axis=1).astype(BF16)
    wuq = jnp.pad(w_uq.reshape(Q_LORA, HB, DB_NOPE + DB_ROPE), ((0, 0), (0, 0), (0, 32))).reshape(Q_LORA, 512)
    ukv = w_ukv.reshape(KV_LORA, HB, DB_NOPE + DB_V)
    wkk = jnp.pad(ukv[:, :, :DB_NOPE], ((0, 0), (0, 0), (0, LANES - DB_NOPE))).reshape(KV_LORA, 512)
    wkv = ukv[:, :, DB_NOPE:].reshape(KV_LORA, HB * DB_V)
    oc = w_out[768:].reshape(HC_KV, HC // HC_KV, DC, D_MODEL).transpose(1, 0, 2, 3).reshape(256, D_MODEL)
    wout = jnp.concatenate([w_out[:768], oc], axis=0).astype(BF16)

    pad96 = lambda g: jnp.pad(g, (0, LANES - DB_NOPE - DB_ROPE))
    gv = jnp.stack([
        jnp.tile(g_qa, 2) * (DA ** -0.5 * LOG2E), jnp.tile(g_ka, 2),
        pad96(g_qb) * ((DB_NOPE + DB_ROPE) ** -0.5 * LOG2E), pad96(g_kb),
        jnp.tile(g_qc, 2) * (DC ** -0.5 * LOG2E), jnp.tile(g_kc, 2),
        g_ckv, jnp.zeros((LANES,), F32)]).astype(F32)

    lam_init = 0.8 - 0.6 * math.exp(-0.3 * layer)
    lam = jnp.exp(jnp.sum(lam_q1 * lam_k1)) - jnp.exp(jnp.sum(lam_q2 * lam_k2)) + lam_init
    lo = (jnp.arange(LANES) < 64).astype(F32)
    hi = 1.0 - lo
    one = jnp.ones((LANES,), F32)
    zero = jnp.zeros((LANES,), F32)
    unit_a = jnp.stack([lo, hi, one, -lam * one, g_suba * (1.0 - lam_init), one, zero, zero])
    unit_b = jnp.stack([one, one, lo, hi, one, zero, zero, zero])
    unit_c = jnp.stack([lo, hi, lo, hi, one, zero, zero, zero])
    up = jnp.stack([unit_a] * 4 + [unit_b] * 2 + [unit_c] * 2).astype(F32)

    wr = jnp.pad(w_router, ((0, 0), (0, LANES - N_EXPERTS)))
    wrh = wr.astype(BF16)
    wrl = (wr - wrh.astype(F32)).astype(BF16)
    def bounded(n, gq, gk):
        return n * jnp.max(jnp.abs(gq)) * jnp.max(jnp.abs(gk)) <= SAFE_LOGIT

    flags = jnp.stack([bounded(DA, gv[0], gv[1])] * HA + [bounded(DB_NOPE + DB_ROPE, gv[2], gv[3])] * 2
                      + [bounded(DC, gv[4], gv[5])] * 2).astype(I32)
    tbl = jnp.concatenate([jnp.asarray(_UNIT_TABLE), flags[None, :]], axis=0)
    return dict(win=win, wuq=wuq.astype(BF16), wkk=wkk.astype(BF16), wkv=wkv.astype(BF16), wout=wout, gv=gv,
                gcq=g_cq.reshape(1, Q_LORA).astype(F32), up=up, wrh=wrh, wrl=wrl, tbl=tbl)


_UNIT_TABLE = np.array([
    [0, 1, 2, 3, 4, 6, 8, 9],
    [0, 1, 2, 3, 5, 7, 8, 9],
    [0, 1, 2, 3, 4, 6, 8, 8],
    [0, 1, 2, 3, 5, 7, 8, 8],
    [0, 1, 2, 3, 4, 5, 6, 6],
], np.int32)


def _tiles(s):
    l_pad = s + SEQ_PAD
    tm = max(t for t in (384, 640, 128) if l_pad % t == 0)
    blocks = l_pad // LANES
    n_chunks = min(n for n in range(1, blocks + 1) if blocks % n == 0 and l_pad // n <= MAX_KEY_CHUNK)
    return tm, n_chunks


def _encode(x, meta_tokens, ln_mix, ln_ffn, layers, w_gate, w_up, w_down):
    b, s, d = x.shape
    l_real = s + N_META
    l_pad = s + SEQ_PAD
    n_pad = b * l_pad
    tm, n_chunks = _tiles(s)
    cap = EC_FACTOR * (b * l_real) // N_EXPERTS
    cap_pad = -(-(cap + COMBINE_WIN) // ROW_ALIGN) * ROW_ALIGN
    ctile = max(t for t in (256, 128) if n_pad % t == 0)
    n_rt = max(1, round(cap_pad / 1100))
    while cap_pad % n_rt or (cap_pad // n_rt) % 16:
        n_rt += 1
    rt = cap_pad // n_rt
    g_rows = max(r for r in range(8, 513, 8) if cap_pad % r == 0)
    nb_pad = -(-(n_pad // LANES) // LANES) * LANES

    h = jnp.concatenate([jnp.broadcast_to(meta_tokens[None], (b, N_META, d)), x,
                         jnp.zeros((b, SEQ_PAD - N_META, d), x.dtype)], axis=1).reshape(n_pad, d)
    rope = _rope_tables(s, l_pad)
    bd = jnp.asarray(np.stack([_block_diag(64), _block_diag(128), _block_diag(256)]), BF16)

    for l, p in enumerate(layers):
        q, k, v = _in_proj(h, ln_mix[l].reshape(1, d), p["win"], p["wuq"], p["wkk"], p["wkv"], p["gv"], p["gcq"],
                           rope, bd, tm=tm, l_pad=l_pad)
        mix = _attention(p["tbl"], q, k, v, p["up"], batch=b, l_pad=l_pad, tq=tm, n_chunks=n_chunks)
        h1, xt, aff = _out_proj(mix, h, p["wout"], ln_ffn[l].reshape(1, d), p["wrh"], p["wrl"], tm=tm,
                                l_real=l_real, l_pad=l_pad)
        aff3 = jnp.pad(aff, ((0, 0), (0, nb_pad * LANES - n_pad)), constant_values=-1.0)
        pos3, off3, idxb = _select(aff3.reshape(N_EXPERTS, nb_pad, LANES), cap=cap, cap_pad=cap_pad, n_tok=n_pad)
        idx = idxb[:, :, 0, :].reshape(N_EXPERTS, cap_pad)
        pos = pos3.reshape(N_EXPERTS, nb_pad * LANES)[:, :n_pad]
        off = off3[:, ::ctile // LANES, 0][:, :n_pad // ctile + 1]
        xe = _gather(idx, xt, rows=g_rows)
        ye = _ffn(xe, w_gate[l], w_up[l], w_down[l], rt=rt)
        h = _combine(off, pos, aff, ye, h1, tile=ctile)
    return h.reshape(b, l_pad, d)[:, N_META:N_META + s]


def kernel(x_prompt, x_sample, meta_tokens, ln_mix, w_in, w_out, g_qa, g_ka, lam_q1, lam_k1, lam_q2, lam_k2, g_suba,
           g_cq, w_uq, g_ckv, w_ukv, g_qb, g_kb, g_qc, g_kc, ln_ffn, w_router, w_gate, w_up, w_down):
    depth = w_in.shape[0]
    layers = [
        _prep_layer(l, w_in[l], w_out[l], g_qa[l], g_ka[l], lam_q1[l], lam_k1[l], lam_q2[l], lam_k2[l], g_suba[l],
                    g_cq[l], w_uq[l], g_ckv[l], w_ukv[l], g_qb[l], g_kb[l], g_qc[l], g_kc[l], w_router[l])
        for l in range(depth)
    ]
    y_prompt = _encode(x_prompt, meta_tokens, ln_mix, ln_ffn, layers, w_gate, w_up, w_down)
    y_sample = _encode(x_sample, meta_tokens, ln_mix, ln_ffn, layers, w_gate, w_up, w_down)
    return (y_prompt, y_sample)
```

```python
import functools
import math

import jax
import jax.numpy as jnp
import numpy as np
from jax import lax
from jax.experimental import pallas as pl
from jax.experimental.pallas import tpu as pltpu

F32 = jnp.float32
BF16 = jnp.bfloat16
I32 = jnp.int32

D_MODEL = 1024
N_META = 16
GRID_W = 64
ROPE_THETA = 10000.0
EPS = 1e-6
HA, DA, VA = 4, 64, 128
HB, Q_LORA, KV_LORA, DB_NOPE, DB_ROPE, DB_V = 4, 256, 128, 64, 32, 64
HC, HC_KV, DC = 4, 2, 64
N_EXPERTS = 16
EC_FACTOR = 2
EXPERT_FF = 2816
LOG2E = math.log2(math.e)

LANES = 128
SEQ_PAD = 128
ZW = 2560
NQ, NK, NV, NU = 10, 9, 7, 8
FF_CHUNK = 256
MAX_KEY_CHUNK = 1792
COMBINE_WIN = 128
ROW_ALIGN = 16
FFN_ROWS = 2100
GATHER_ROWS = 704
NEG = -1e30
SAFE_LOGIT = 60.0
VMEM_LIMIT = 56 * 1024 * 1024


def _cparams(sem):
    return pltpu.CompilerParams(dimension_semantics=sem, vmem_limit_bytes=VMEM_LIMIT)


def _seg_rms(x, bd, inv_n):
    ss = jnp.dot((x * x).astype(BF16), bd, preferred_element_type=F32)
    return x * lax.rsqrt(ss * inv_n + EPS)


def _in_proj_kernel(h_ref, ln_ref, win_ref, wuq_ref, wkk_ref, wkv_ref, gv_ref, gcq_ref, rope_ref, bd_ref,
                    q_ref, k_ref, v_ref):
    h = h_ref[...]
    ms = jnp.mean(h * h, axis=-1, keepdims=True)
    u = (h * lax.rsqrt(ms + EPS) * ln_ref[...]).astype(BF16)
    z = jnp.dot(u, win_ref[...], preferred_element_type=F32)
    bd64, bd128, ones256 = bd_ref[0], bd_ref[1], bd_ref[2]

    def rope(x, t, sh):
        return (x * rope_ref[3 * t] + pltpu.roll(x, LANES - sh, 1) * rope_ref[3 * t + 1]
                + pltpu.roll(x, sh, 1) * rope_ref[3 * t + 2])

    def blk(x, j):
        return x[:, j * LANES:(j + 1) * LANES]

    for s in range(2):
        xq = _seg_rms(z[:, s * 256:(s + 1) * 256], bd64, 1.0 / DA)
        xk = _seg_rms(z[:, 512 + s * 256:512 + (s + 1) * 256], bd64, 1.0 / DA)
        for j in range(2):
            q_ref[2 * s + j] = rope(blk(xq, j) * gv_ref[0:1, :], 0, 32).astype(BF16)
            k_ref[2 * s + j] = rope(blk(xk, j) * gv_ref[1:2, :], 0, 32).astype(BF16)
    for j in range(4):
        v_ref[j] = blk(z, 8 + j).astype(BF16)

    cq = (_seg_rms(z[:, 1536:1792], ones256, 1.0 / Q_LORA) * gcq_ref[...]).astype(BF16)
    qb = jnp.dot(cq, wuq_ref[...], preferred_element_type=F32)
    ckv = (_seg_rms(z[:, 1792:1920], bd128[:LANES, :LANES], 1.0 / KV_LORA) * gv_ref[6:7, :]).astype(BF16)
    kk = jnp.dot(ckv, wkk_ref[...], preferred_element_type=F32)
    vb = jnp.dot(ckv, wkv_ref[...], preferred_element_type=F32)
    kr = pltpu.roll(z[:, 2432:2560], 64, 1)
    for s in range(2):
        xq = _seg_rms(qb[:, s * 256:(s + 1) * 256], bd128, 1.0 / (DB_NOPE + DB_ROPE))
        kraw = kk[:, s * 256:(s + 1) * 256] + jnp.concatenate([kr, kr], axis=1)
        xk = _seg_rms(kraw, bd128, 1.0 / (DB_NOPE + DB_ROPE))
        for j in range(2):
            q_ref[4 + 2 * s + j] = rope(blk(xq, j) * gv_ref[2:3, :], 1, 16).astype(BF16)
            k_ref[4 + 2 * s + j] = rope(blk(xk, j) * gv_ref[3:4, :], 1, 16).astype(BF16)
        v_ref[4 + s] = vb[:, s * LANES:(s + 1) * LANES].astype(BF16)

    xq = _seg_rms(z[:, 1920:2176], bd64, 1.0 / DC)
    for j in range(2):
        q_ref[8 + j] = rope(blk(xq, j) * gv_ref[4:5, :], 2, 16).astype(BF16)
    xk = _seg_rms(z[:, 2176:2304], bd64[:LANES, :LANES], 1.0 / DC)
    k_ref[8] = rope(xk * gv_ref[5:6, :], 2, 16).astype(BF16)
    v_ref[6] = z[:, 2304:2432].astype(BF16)


def _in_proj(h, ln, win, wuq, wkk, wkv, gv, gcq, rope, bd, *, tm, l_pad):
    n_pad = h.shape[0]
    tiles_per_seq = l_pad // tm
    full = lambda shape: pl.BlockSpec(shape, lambda i: (0,) * len(shape))
    return pl.pallas_call(
        _in_proj_kernel,
        grid=(n_pad // tm,),
        in_specs=[
            pl.BlockSpec((tm, D_MODEL), lambda i: (i, 0)),
            full((1, D_MODEL)), full((D_MODEL, ZW)), full((Q_LORA, 512)), full((KV_LORA, 512)),
            full((KV_LORA, 256)), full((8, LANES)), full((1, Q_LORA)),
            pl.BlockSpec((9, tm, LANES), lambda i: (0, i % tiles_per_seq, 0)),
            full((3, 256, 256)),
        ],
        out_specs=[
            pl.BlockSpec((NQ, tm, LANES), lambda i: (0, i, 0)),
            pl.BlockSpec((NK, tm, LANES), lambda i: (0, i, 0)),
            pl.BlockSpec((NV, tm, LANES), lambda i: (0, i, 0)),
        ],
        out_shape=[
            jax.ShapeDtypeStruct((NQ, n_pad, LANES), BF16),
            jax.ShapeDtypeStruct((NK, n_pad, LANES), BF16),
            jax.ShapeDtypeStruct((NV, n_pad, LANES), BF16),
        ],
        compiler_params=_cparams(("parallel",)),
        name="in_proj",
    )(h, ln, win, wuq, wkk, wkv, gv, gcq, rope, bd)


def _attn_kernel(tbl_ref, q0_ref, q1_ref, k0_ref, k1_ref, v_ref, up_ref, o_ref, *, n_chunks, tk, tq, l_real):
    up = up_ref[0]
    q0 = (q0_ref[0].astype(F32) * up[0:1, :]).astype(BF16)
    q1 = (q1_ref[0].astype(F32) * up[1:2, :]).astype(BF16)
    nt = (((1,), (1,)), ((), ()))
    last_mask = lax.broadcasted_iota(I32, (tq, tk), 1) < l_real - (n_chunks - 1) * tk

    def finish(a0, l0, a1, l1):
        o = up[2:3, :] * (a0 / l0) + up[3:4, :] * (a1 / l1)
        r = lax.rsqrt(jnp.mean(o * o, axis=-1, keepdims=True) + EPS)
        use = up[5:6, :]
        o_ref[...] = (o * (use * r + (1.0 - use)) * up[4:5, :]).astype(BF16)

    def chunks(step, init, unroll):
        if unroll:
            carry = init
            for c in range(n_chunks - 1):
                carry = step(c * tk, tk, carry, None)
        else:
            carry = lax.fori_loop(0, n_chunks - 1, lambda c, cr: step(pl.multiple_of(c * tk, tk), tk, cr, None),
                                  init)
        return step((n_chunks - 1) * tk, tk, carry, last_mask)

    @pl.when(tbl_ref[5, pl.program_id(1)] != 0)
    def _():
        def one(q, kc, vc, l, a, mask):
            s = lax.dot_general(q, kc, nt, preferred_element_type=F32)
            if mask is not None:
                s = jnp.where(mask, s, NEG)
            p = jnp.exp2(s)
            for j in range(p.shape[1] // LANES):
                l = l + p[:, j * LANES:(j + 1) * LANES]
            return l, a + jnp.dot(p.astype(BF16), vc, preferred_element_type=F32)

        def step(start, size, carry, mask):
            l0, a0, l1, a1 = carry
            vc = v_ref[0, 0, pl.ds(start, size), :]
            l0, a0 = one(q0, k0_ref[0, 0, pl.ds(start, size), :], vc, l0, a0, mask)
            l1, a1 = one(q1, k1_ref[0, 0, pl.ds(start, size), :], vc, l1, a1, mask)
            return l0, a0, l1, a1

        l0, a0, l1, a1 = chunks(step, (jnp.zeros((tq, LANES), F32),) * 4, True)
        finish(a0, jnp.sum(l0, axis=-1, keepdims=True), a1, jnp.sum(l1, axis=-1, keepdims=True))

    @pl.when(tbl_ref[5, pl.program_id(1)] == 0)
    def _():
        def one(q, kc, vc, m, l, a, mask):
            s = lax.dot_general(q, kc, nt, preferred_element_type=F32)
            if mask is not None:
                s = jnp.where(mask, s, NEG)
            mn = jnp.maximum(m, jnp.max(s, axis=-1, keepdims=True))
            al = jnp.exp2(m - mn)
            p = jnp.exp2(s - mn)
            l = al * l + jnp.sum(p, axis=-1, keepdims=True)
            a = al * a + jnp.dot(p.astype(BF16), vc, preferred_element_type=F32)
            return mn, l, a

        def step(start, size, carry, mask):
            m0, l0, a0, m1, l1, a1 = carry
            vc = v_ref[0, 0, pl.ds(start, size), :]
            m0, l0, a0 = one(q0, k0_ref[0, 0, pl.ds(start, size), :], vc, m0, l0, a0, mask)
            m1, l1, a1 = one(q1, k1_ref[0, 0, pl.ds(start, size), :], vc, m1, l1, a1, mask)
            return m0, l0, a0, m1, l1, a1

        init = (jnp.full((tq, 1), NEG, F32), jnp.zeros((tq, 1), F32), jnp.zeros((tq, LANES), F32)) * 2
        _, l0, a0, _, l1, a1 = chunks(step, init, False)
        finish(a0, l0, a1, l1)


def _attention(tbl, q, k, v, up, *, batch, l_pad, tq, n_chunks):
    n_pad = q.shape[1]
    nq = l_pad // tq
    tk = l_pad // n_chunks
    k4 = k.reshape(NK, batch, l_pad, LANES)
    v4 = v.reshape(NV, batch, l_pad, LANES)
    qspec = lambda row: pl.BlockSpec((1, tq, LANES), lambda b, u, i, t: (t[row, u], b * nq + i, 0))
    kspec = lambda row: pl.BlockSpec((1, 1, l_pad, LANES), lambda b, u, i, t: (t[row, u], b, 0, 0))
    grid_spec = pltpu.PrefetchScalarGridSpec(
        num_scalar_prefetch=1,
        grid=(batch, NU, nq),
        in_specs=[qspec(0), qspec(1), kspec(2), kspec(3), kspec(4),
                  pl.BlockSpec((1, 8, LANES), lambda b, u, i, t: (u, 0, 0))],
        out_specs=pl.BlockSpec((tq, LANES), lambda b, u, i, t: (b * nq + i, u)),
    )
    return pl.pallas_call(
        functools.partial(_attn_kernel, n_chunks=n_chunks, tk=tk, tq=tq, l_real=l_pad - SEQ_PAD + N_META),
        grid_spec=grid_spec,
        out_shape=jax.ShapeDtypeStruct((n_pad, NU * LANES), BF16),
        compiler_params=_cparams(("parallel", "parallel", "arbitrary")),
        name="attention",
    )(tbl, q, q, k4, k4, v4, up)


def _out_proj_kernel(mix_ref, h_ref, wout_ref, ln_ref, wrh_ref, wrl_ref, h1_ref, xt_ref, aff_ref, *, tm, l_real,
                     l_pad):
    h1 = h_ref[...] + jnp.dot(mix_ref[...], wout_ref[...], preferred_element_type=F32)
    h1_ref[...] = h1
    ms = jnp.mean(h1 * h1, axis=-1, keepdims=True)
    xt = h1 * lax.rsqrt(ms + EPS) * ln_ref[...]
    for j in range(D_MODEL // LANES):
        xt_ref[pl.ds(j, tm, stride=D_MODEL // LANES), :] = xt[:, j * LANES:(j + 1) * LANES]
    xh = xt.astype(BF16)
    xl = (xt - xh.astype(F32)).astype(BF16)
    lg = (jnp.dot(xh, wrh_ref[...], preferred_element_type=F32)
          + jnp.dot(xl, wrh_ref[...], preferred_element_type=F32)
          + jnp.dot(xh, wrl_ref[...], preferred_element_type=F32))
    lt = lg.T[:N_EXPERTS, :]
    mx = jnp.max(lt, axis=0, keepdims=True)
    ex = jnp.exp(lt - mx)
    aff = ex / jnp.sum(ex, axis=0, keepdims=True)
    row0 = (pl.program_id(0) % (l_pad // tm)) * tm
    row = row0 + lax.broadcasted_iota(I32, (N_EXPERTS, tm), 1)
    aff_ref[...] = jnp.where(row < l_real, aff, -1.0)


def _out_proj(mix, h, wout, ln, wrh, wrl, *, tm, l_real, l_pad):
    n_pad = h.shape[0]
    full = lambda shape: pl.BlockSpec(shape, lambda i: (0,) * len(shape))
    return pl.pallas_call(
        functools.partial(_out_proj_kernel, tm=tm, l_real=l_real, l_pad=l_pad),
        grid=(n_pad // tm,),
        in_specs=[
            pl.BlockSpec((tm, D_MODEL), lambda i: (i, 0)),
            pl.BlockSpec((tm, D_MODEL), lambda i: (i, 0)),
            full((D_MODEL, D_MODEL)), full((1, D_MODEL)), full((D_MODEL, LANES)), full((D_MODEL, LANES)),
        ],
        out_specs=[
            pl.BlockSpec((tm, D_MODEL), lambda i: (i, 0)),
            pl.BlockSpec((tm * (D_MODEL // LANES), LANES), lambda i: (i, 0)),
            pl.BlockSpec((N_EXPERTS, tm), lambda i: (0, i)),
        ],
        out_shape=[
            jax.ShapeDtypeStruct((n_pad, D_MODEL), F32),
            jax.ShapeDtypeStruct((n_pad * (D_MODEL // LANES), LANES), F32),
            jax.ShapeDtypeStruct((N_EXPERTS, n_pad), F32),
        ],
        compiler_params=_cparams(("parallel",)),
        name="out_proj",
    )(mix, h, wout, ln, wrh, wrl)


def _select_kernel(aff_ref, pos_ref, off_ref, idx_ref, *, cap, nb, n_chunks, n_tok):
    li = lax.broadcasted_iota(I32, (LANES, LANES), 0)
    lj = lax.broadcasted_iota(I32, (LANES, LANES), 1)
    upper_incl = (li <= lj).astype(BF16)
    ones_l = jnp.ones((LANES, LANES), BF16)
    bi = lax.broadcasted_iota(I32, (nb, nb), 0)
    bj = lax.broadcasted_iota(I32, (nb, nb), 1)
    lower_strict = (bj < bi).astype(BF16)
    upper_strict = (bi < bj).astype(BF16)
    ones8 = jnp.ones((8, LANES), BF16)
    blk_lane = lax.broadcasted_iota(I32, (LANES, nb), 1)
    slot_sub = lax.broadcasted_iota(I32, (LANES, nb), 0)
    slot_sub_l = lax.broadcasted_iota(I32, (LANES, LANES), 0)

    def prefix(mask_bf):
        pre = jnp.dot(mask_bf, upper_incl, preferred_element_type=F32)
        tot = jnp.dot(mask_bf, ones_l, preferred_element_type=F32)
        off = jnp.dot(lower_strict, tot.astype(BF16), preferred_element_type=F32)
        return pre, tot, off

    def per_expert(e, carry):
        a = aff_ref[e]
        bits = pltpu.bitcast(a, I32)

        def search(i, t):
            cand = t | lax.shift_left(jnp.int32(1), 30 - i)
            cnt = jnp.sum((bits >= cand).astype(I32), axis=(0, 1), keepdims=True)
            return jnp.where(cnt >= cap, cand, t)

        thr = lax.fori_loop(0, 31, search, jnp.zeros((1, 1), I32))
        gt = bits > thr
        eq = bits == thr
        need = (cap - jnp.sum(gt.astype(I32), axis=(0, 1), keepdims=True)).astype(F32)
        pre_e, _, off_e = prefix(jnp.where(eq, 1.0, 0.0).astype(BF16))
        sel = gt | (eq & (pre_e + off_e <= need))
        sel_bf = jnp.where(sel, 1.0, 0.0).astype(BF16)
        pre, tot, off = prefix(sel_bf)
        pos_ref[e] = jnp.where(sel, (pre + off).astype(I32) - 1, -1)
        off_ref[e] = off.astype(I32)

        tot_row = lax.dot_general(ones8, sel_bf, (((1,), (1,)), ((), ())), preferred_element_type=F32)
        off_row = jnp.dot(tot_row.astype(BF16), upper_strict, preferred_element_type=F32)[0:1, :]
        nxt_row = off_row + tot_row[0:1, :]
        pre_bf = pre.astype(BF16)

        def per_chunk(c, carry2):
            r = slot_sub + c * LANES
            nfull = jnp.sum((nxt_row <= r.astype(F32)).astype(I32), axis=1, keepdims=True)
            hot = blk_lane == nfull
            rows = jnp.dot(jnp.where(hot, 1.0, 0.0).astype(BF16), pre_bf, preferred_element_type=F32)
            base = jnp.sum(jnp.where(hot, off_row, 0.0), axis=1, keepdims=True)
            rem = (slot_sub_l[:, 0:1] + c * LANES).astype(F32) - base
            part = jnp.sum((rows <= rem).astype(I32), axis=1, keepdims=True)
            tok = jnp.minimum(nfull * LANES + part, n_tok - 1)
            idx_ref[e, c] = jnp.broadcast_to(tok, (LANES, LANES)).T[:8, :]
            return carry2

        lax.fori_loop(0, n_chunks, per_chunk, 0, unroll=2)
        return carry

    lax.fori_loop(0, N_EXPERTS, per_expert, 0)


def _select(aff3, *, cap, cap_pad, n_tok):
    nb = aff3.shape[1]
    n_chunks = 2 * (-(-cap_pad // (2 * LANES)))
    return pl.pallas_call(
        functools.partial(_select_kernel, cap=cap, nb=nb, n_chunks=n_chunks, n_tok=n_tok),
        out_shape=[
            jax.ShapeDtypeStruct((N_EXPERTS, nb, LANES), I32),
            jax.ShapeDtypeStruct((N_EXPERTS, nb, LANES), I32),
            jax.ShapeDtypeStruct((N_EXPERTS, n_chunks, 8, LANES), I32),
        ],
        compiler_params=pltpu.CompilerParams(vmem_limit_bytes=VMEM_LIMIT),
        name="select",
    )(aff3)


def _gather_kernel(idx_ref, nxt_ref, x_hbm, o_hbm, buf, obuf, gsem, wsem, *, rows, n_steps):
    s = pl.program_id(0)
    slot = s % 2
    other = 1 - slot
    sub = D_MODEL // LANES

    def start_rows(ids_ref, dst_slot):
        def start(g, c):
            for j in range(8):
                i = g * 8 + j
                src = x_hbm.at[pl.ds(pl.multiple_of(ids_ref[0, 0, i] * sub, sub), sub)]
                pltpu.make_async_copy(src, buf.at[dst_slot, pl.ds(i * sub, sub)],
                                      gsem.at[dst_slot]).start(priority=j % 2)
            return c
        lax.fori_loop(0, rows // 8, start, 0)

    def writeback(src_slot, step):
        return pltpu.make_async_copy(obuf.at[src_slot], o_hbm.at[pl.ds(step * rows, rows)], wsem.at[src_slot])

    @pl.when(s == 0)
    def _():
        start_rows(idx_ref, slot)

    @pl.when(s + 1 < n_steps)
    def _():
        start_rows(nxt_ref, other)

    pltpu.make_async_copy(x_hbm.at[pl.ds(0, rows * sub)], buf.at[slot], gsem.at[slot]).wait()

    @pl.when(s >= 2)
    def _():
        writeback(slot, s - 2).wait()

    for j in range(sub):
        obuf[slot, :, j * LANES:(j + 1) * LANES] = buf[slot, pl.ds(j, rows, stride=sub), :].astype(BF16)
    writeback(slot, s).start()

    @pl.when(s == n_steps - 1)
    def _():
        writeback(slot, s).wait()

    @pl.when(jnp.logical_and(s == n_steps - 1, n_steps >= 2))
    def _():
        writeback(other, s - 1).wait()


def _gather(idx, xt, *, rows):
    e, cap_pad = idx.shape
    n_steps = e * cap_pad // rows
    sub = D_MODEL // LANES
    xt3 = xt.reshape(-1, LANES)
    ids = idx.reshape(n_steps, 1, rows)
    out = pl.pallas_call(
        functools.partial(_gather_kernel, rows=rows, n_steps=n_steps),
        grid=(n_steps,),
        in_specs=[
            pl.BlockSpec((1, 1, rows), lambda s: (s, 0, 0), memory_space=pltpu.SMEM),
            pl.BlockSpec((1, 1, rows), lambda s: (jnp.minimum(s + 1, n_steps - 1), 0, 0), memory_space=pltpu.SMEM),
            pl.BlockSpec(memory_space=pl.ANY),
        ],
        out_specs=pl.BlockSpec(memory_space=pl.ANY),
        out_shape=jax.ShapeDtypeStruct((e * cap_pad, D_MODEL), BF16),
        scratch_shapes=[pltpu.VMEM((2, rows * sub, LANES), F32), pltpu.VMEM((2, rows, D_MODEL), BF16),
                        pltpu.SemaphoreType.DMA((2,)), pltpu.SemaphoreType.DMA((2,))],
        compiler_params=_cparams(("arbitrary",)),
        name="gather",
    )(ids, ids, xt3)
    return out.reshape(e, cap_pad, D_MODEL)


def _ffn_kernel(x_ref, wg_ref, wu_ref, wd_ref, o_ref, acc_ref):
    f = pl.program_id(2)

    @pl.when(f == 0)
    def _():
        acc_ref[...] = jnp.zeros_like(acc_ref)

    x = x_ref[0]
    g = jnp.dot(x, wg_ref[0].astype(BF16), preferred_element_type=F32)
    u = jnp.dot(x, wu_ref[0].astype(BF16), preferred_element_type=F32)
    hid = (g / (1.0 + jnp.exp(-g)) * u).astype(BF16)
    acc_ref[...] += jnp.dot(hid, wd_ref[0].astype(BF16), preferred_element_type=F32)

    @pl.when(f == pl.num_programs(2) - 1)
    def _():
        o_ref[0] = acc_ref[...].astype(BF16)


def _ffn(xe, w_gate, w_up, w_down, *, rt):
    e, cap_pad = xe.shape[:2]
    return pl.pallas_call(
        _ffn_kernel,
        grid=(e, cap_pad // rt, EXPERT_FF // FF_CHUNK),
        in_specs=[
            pl.BlockSpec((1, rt, D_MODEL), lambda i, r, f: (i, r, 0)),
            pl.BlockSpec((1, D_MODEL, FF_CHUNK), lambda i, r, f: (i, 0, f)),
            pl.BlockSpec((1, D_MODEL, FF_CHUNK), lambda i, r, f: (i, 0, f)),
            pl.BlockSpec((1, FF_CHUNK, D_MODEL), lambda i, r, f: (i, f, 0)),
        ],
        out_specs=pl.BlockSpec((1, rt, D_MODEL), lambda i, r, f: (i, r, 0)),
        out_shape=jax.ShapeDtypeStruct((e, cap_pad, D_MODEL), BF16),
        scratch_shapes=[pltpu.VMEM((rt, D_MODEL), F32)],
        compiler_params=_cparams(("parallel", "parallel", "arbitrary")),
        name="ffn",
    )(xe, w_gate, w_up, w_down)


def _combine_kernel(off_ref, pos_ref, aff_ref, ye_hbm, h_ref, o_ref, yw_ref, st_ref, sem, *, tile):
    t = pl.program_id(0)
    n_tiles = pl.num_programs(0)
    buf = t % 2

    @pl.when(t == 0)
    def _():
        yw_ref[...] = jnp.zeros_like(yw_ref)

    o_ref[...] = h_ref[...]

    last_start = ye_hbm.shape[1] - COMBINE_WIN

    def window(tt, e, k):
        base = off_ref[e, tt]
        end = off_ref[e, tt + 1]
        first = lax.shift_left(lax.shift_right_logical(base, 4), 4) + k * COMBINE_WIN
        return first, jnp.minimum(first, last_start), jnp.logical_and(end > base, end > first)

    def copy(e, start, b):
        return pltpu.make_async_copy(ye_hbm.at[e, pl.ds(pl.multiple_of(start, 16), COMBINE_WIN), :],
                                     yw_ref.at[b, pl.ds(e * COMBINE_WIN, COMBINE_WIN), :], sem.at[b])

    def start_round(tt, k, b):
        for e in range(N_EXPERTS):
            _, start, active = window(tt, e, k)
            pl.when(active)(copy(e, start, b).start)

    slot = lax.broadcasted_iota(I32, (COMBINE_WIN, tile), 0)

    def finish_round(k):
        for e in range(N_EXPERTS):
            first, start, active = window(t, e, k)
            start = jnp.where(active, start, -(1 << 30))
            pos = pos_ref[e:e + 1, :]
            pos = jnp.where(pos >= first, pos, -1)
            st_ref[e * COMBINE_WIN:(e + 1) * COMBINE_WIN, :] = jnp.where(
                pos - start == slot, aff_ref[e:e + 1, :], 0.0).astype(BF16)
        for e in range(N_EXPERTS):
            _, start, active = window(t, e, k)
            pl.when(active)(copy(e, start, buf).wait)
        o_ref[...] += lax.dot_general(st_ref[...], yw_ref[buf], (((0,), (0,)), ((), ())),
                                      preferred_element_type=F32)

    @pl.when(t == 0)
    def _():
        start_round(t, 0, buf)

    @pl.when(t + 1 < n_tiles)
    def _():
        start_round(t + 1, 0, 1 - buf)

    finish_round(0)

    n_rounds = jnp.int32(0)
    for e in range(N_EXPERTS):
        first, _, _ = window(t, e, 0)
        span = jnp.where(off_ref[e, t + 1] > off_ref[e, t], off_ref[e, t + 1] - first, 0)
        n_rounds = jnp.maximum(n_rounds, lax.shift_right_logical(span + (COMBINE_WIN - 1), 7))

    def extra_round(k, c):
        start_round(t, k, buf)
        finish_round(k)
        return c

    lax.fori_loop(1, n_rounds, extra_round, 0)


def _combine(off, pos, aff, ye, h1, *, tile):
    n_pad = h1.shape[0]
    e = ye.shape[0]
    grid_spec = pltpu.PrefetchScalarGridSpec(
        num_scalar_prefetch=1,
        grid=(n_pad // tile,),
        in_specs=[
            pl.BlockSpec((e, tile), lambda t, o: (0, t)),
            pl.BlockSpec((e, tile), lambda t, o: (0, t)),
            pl.BlockSpec(memory_space=pl.ANY),
            pl.BlockSpec((tile, D_MODEL), lambda t, o: (t, 0)),
        ],
        out_specs=pl.BlockSpec((tile, D_MODEL), lambda t, o: (t, 0)),
        scratch_shapes=[pltpu.VMEM((2, e * COMBINE_WIN, D_MODEL), BF16), pltpu.VMEM((e * COMBINE_WIN, tile), BF16),
                        pltpu.SemaphoreType.DMA((2,))],
    )
    return pl.pallas_call(
        functools.partial(_combine_kernel, tile=tile),
        grid_spec=grid_spec,
        out_shape=jax.ShapeDtypeStruct((n_pad, D_MODEL), F32),
        compiler_params=_cparams(("arbitrary",)),
        name="combine",
    )(off, pos, aff, ye, h1)


def _rope_tables(s, l_pad):
    del s
    l = np.arange(l_pad)
    real = l - N_META
    pos = l.astype(np.float64)
    row = np.where(l < N_META, -1, real // GRID_W).astype(np.float64)
    col = np.where(l < N_META, l, real % GRID_W).astype(np.float64)

    def cos_sin(p, half):
        ang = p[:, None] * (ROPE_THETA ** (-np.arange(half, dtype=np.float64) / half))[None, :]
        return jnp.asarray(np.cos(ang), F32), jnp.asarray(np.sin(ang), F32)

    def tables(groups):
        c, s1, s2 = [], [], []
        for g in groups:
            if isinstance(g, int):
                c.append(jnp.ones((l_pad, g), F32))
                s1.append(jnp.zeros((l_pad, g), F32))
                s2.append(jnp.zeros((l_pad, g), F32))
            else:
                co, si = g
                z = jnp.zeros_like(si)
                c += [co, co]
                s1 += [-si, z]
                s2 += [z, si]
        return [jnp.concatenate(t, axis=1) for t in (c, s1, s2)]

    a = cos_sin(pos, 32)
    b = cos_sin(pos, 16)
    rr, cc = cos_sin(row, 16), cos_sin(col, 16)
    return jnp.stack(tables([a, a]) + tables([64, b, 32]) + tables([rr, cc, rr, cc]))


def _block_diag(seg):
    i = np.arange(256)
    return (i[:, None] // seg == i[None, :] // seg).astype(np.float32)


def _prep_layer(layer, w_in, w_out, g_qa, g_ka, lam_q1, lam_k1, lam_q2, lam_k2, g_suba, g_cq, w_uq, g_ckv, w_ukv,
                g_qb, g_kb, g_qc, g_kc, w_router):
    zq = w_in[:, 1952:2208].reshape(D_MODEL, HC_KV, HC // HC_KV, DC).transpose(0, 2, 1, 3).reshape(D_MODEL, 256)
    kr = jnp.pad(w_in[:, 1920:1952], ((0, 0), (0, LANES - DB_ROPE)))
    win = jnp.concatenate([w_in[:, :1920], zq, w_in[:, 2208:2464], kr], axis=1).astype(BF16)
    wuq = jnp.pad(w_uq.reshape(Q_LORA, HB, DB_NOPE + DB_ROPE), ((0, 0), (0, 0), (0, 32))).reshape(Q_LORA, 512)
    ukv = w_ukv.reshape(KV_LORA, HB, DB_NOPE + DB_V)
    wkk = jnp.pad(ukv[:, :, :DB_NOPE], ((0, 0), (0, 0), (0, LANES - DB_NOPE))).reshape(KV_LORA, 512)
    wkv = ukv[:, :, DB_NOPE:].reshape(KV_LORA, HB * DB_V)
    oc = w_out[768:].reshape(HC_KV, HC // HC_KV, DC, D_MODEL).transpose(1, 0, 2, 3).reshape(256, D_MODEL)
    wout = jnp.concatenate([w_out[:768], oc], axis=0).astype(BF16)

    pad96 = lambda g: jnp.pad(g, (0, LANES - DB_NOPE - DB_ROPE))
    gv = jnp.stack([
        jnp.tile(g_qa, 2) * (DA ** -0.5 * LOG2E), jnp.tile(g_ka, 2),
        pad96(g_qb) * ((DB_NOPE + DB_ROPE) ** -0.5 * LOG2E), pad96(g_kb),
        jnp.tile(g_qc, 2) * (DC ** -0.5 * LOG2E), jnp.tile(g_kc, 2),
        g_ckv, jnp.zeros((LANES,), F32)]).astype(F32)

    lam_init = 0.8 - 0.6 * math.exp(-0.3 * layer)
    lam = jnp.exp(jnp.sum(lam_q1 * lam_k1)) - jnp.exp(jnp.sum(lam_q2 * lam_k2)) + lam_init
    lo = (jnp.arange(LANES) < 64).astype(F32)
    hi = 1.0 - lo
    one = jnp.ones((LANES,), F32)
    zero = jnp.zeros((LANES,), F32)
    unit_a = jnp.stack([lo, hi, one, -lam * one, g_suba * (1.0 - lam_init), one, zero, zero])
    unit_b = jnp.stack([one, one, lo, hi, one, zero, zero, zero])
    unit_c = jnp.stack([lo, hi, lo, hi, one, zero, zero, zero])
    up = jnp.stack([unit_a] * 4 + [unit_b] * 2 + [unit_c] * 2).astype(F32)

    wr = jnp.pad(w_router, ((0, 0), (0, LANES - N_EXPERTS)))
    wrh = wr.astype(BF16)
    wrl = (wr - wrh.astype(F32)).astype(BF16)
    def bounded(n, gq, gk):
        return n * jnp.max(jnp.abs(gq)) * jnp.max(jnp.abs(gk)) <= SAFE_LOGIT

    flags = jnp.stack([bounded(DA, gv[0], gv[1])] * HA + [bounded(DB_NOPE + DB_ROPE, gv[2], gv[3])] * 2
                      + [bounded(DC, gv[4], gv[5])] * 2).astype(I32)
    tbl = jnp.concatenate([jnp.asarray(_UNIT_TABLE), flags[None, :]], axis=0)
    return dict(win=win, wuq=wuq.astype(BF16), wkk=wkk.astype(BF16), wkv=wkv.astype(BF16), wout=wout, gv=gv,
                gcq=g_cq.reshape(1, Q_LORA).astype(F32), up=up, wrh=wrh, wrl=wrl, tbl=tbl)


_UNIT_TABLE = np.array([
    [0, 1, 2, 3, 4, 6, 8, 9],
    [0, 1, 2, 3, 5, 7, 8, 9],
    [0, 1, 2, 3, 4, 6, 8, 8],
    [0, 1, 2, 3, 5, 7, 8, 8],
    [0, 1, 2, 3, 4, 5, 6, 6],
], np.int32)


def _tiles(s):
    l_pad = s + SEQ_PAD
    tm = max(t for t in (384, 640, 128) if l_pad % t == 0)
    blocks = l_pad // LANES
    n_chunks = min(n for n in range(1, blocks + 1) if blocks % n == 0 and l_pad // n <= MAX_KEY_CHUNK)
    return tm, n_chunks


def _route_tiles(cap):
    n_rt = max(1, round(cap / FFN_ROWS))
    rt = max(-(-(-(-cap // n_rt)) // ROW_ALIGN) * ROW_ALIGN, COMBINE_WIN)
    cap_pad = n_rt * rt
    g_rows = max(r for r in range(ROW_ALIGN, GATHER_ROWS + 1, ROW_ALIGN) if cap_pad % r == 0)
    return rt, cap_pad, g_rows


def _encode(x, meta_tokens, ln_mix, ln_ffn, layers, w_gate, w_up, w_down):
    b, s, d = x.shape
    l_real = s + N_META
    l_pad = s + SEQ_PAD
    n_pad = b * l_pad
    tm, n_chunks = _tiles(s)
    cap = EC_FACTOR * (b * l_real) // N_EXPERTS
    ctile = max(t for t in (256, 128) if n_pad % t == 0)
    rt, cap_pad, g_rows = _route_tiles(cap)
    nb_pad = -(-(n_pad // LANES) // LANES) * LANES

    h = jnp.concatenate([jnp.broadcast_to(meta_tokens[None], (b, N_META, d)), x,
                         jnp.zeros((b, SEQ_PAD - N_META, d), x.dtype)], axis=1).reshape(n_pad, d)
    rope = _rope_tables(s, l_pad)
    bd = jnp.asarray(np.stack([_block_diag(64), _block_diag(128), _block_diag(256)]), BF16)

    for l, p in enumerate(layers):
        q, k, v = _in_proj(h, ln_mix[l].reshape(1, d), p["win"], p["wuq"], p["wkk"], p["wkv"], p["gv"], p["gcq"],
                           rope, bd, tm=tm, l_pad=l_pad)
        mix = _attention(p["tbl"], q, k, v, p["up"], batch=b, l_pad=l_pad, tq=tm, n_chunks=n_chunks)
        h1, xt, aff = _out_proj(mix, h, p["wout"], ln_ffn[l].reshape(1, d), p["wrh"], p["wrl"], tm=tm,
                                l_real=l_real, l_pad=l_pad)
        aff3 = jnp.pad(aff, ((0, 0), (0, nb_pad * LANES - n_pad)), constant_values=-1.0)
        pos3, off3, idxb = _select(aff3.reshape(N_EXPERTS, nb_pad, LANES), cap=cap, cap_pad=cap_pad, n_tok=n_pad)
        idx = idxb[:, :, 0, :].reshape(N_EXPERTS, -1)[:, :cap_pad]
        pos = pos3.reshape(N_EXPERTS, nb_pad * LANES)[:, :n_pad]
        off = off3[:, ::ctile // LANES, 0][:, :n_pad // ctile + 1]
        xe = _gather(idx, xt, rows=g_rows)
        ye = _ffn(xe, w_gate[l], w_up[l], w_down[l], rt=rt)
        h = _combine(off, pos, aff, ye, h1, tile=ctile)
    return h.reshape(b, l_pad, d)[:, N_META:N_META + s]


def kernel(x_prompt, x_sample, meta_tokens, ln_mix, w_in, w_out, g_qa, g_ka, lam_q1, lam_k1, lam_q2, lam_k2, g_suba,
           g_cq, w_uq, g_ckv, w_ukv, g_qb, g_kb, g_qc, g_kc, ln_ffn, w_router, w_gate, w_up, w_down):
    depth = w_in.shape[0]
    layers = [
        _prep_layer(l, w_in[l], w_out[l], g_qa[l], g_ka[l], lam_q1[l], lam_k1[l], lam_q2[l], lam_k2[l], g_suba[l],
                    g_cq[l], w_uq[l], g_ckv[l], w_ukv[l], g_qb[l], g_kb[l], g_qc[l], g_kc[l], w_router[l])
        for l in range(depth)
    ]
    y_prompt = _encode(x_prompt, meta_tokens, ln_mix, ln_ffn, layers, w_gate, w_up, w_down)
    y_sample = _encode(x_sample, meta_tokens, ln_mix, ln_ffn, layers, w_gate, w_up, w_down)
    return (y_prompt, y_sample)
```

```python
import functools
import math

import jax
import jax.numpy as jnp
import numpy as np
from jax import lax
from jax.experimental import pallas as pl
from jax.experimental.pallas import tpu as pltpu

F32 = jnp.float32
BF16 = jnp.bfloat16
I32 = jnp.int32

D_MODEL = 1024
N_META = 16
GRID_W = 64
ROPE_THETA = 10000.0
EPS = 1e-6
HA, DA, VA = 4, 64, 128
HB, Q_LORA, KV_LORA, DB_NOPE, DB_ROPE, DB_V = 4, 256, 128, 64, 32, 64
HC, HC_KV, DC = 4, 2, 64
N_EXPERTS = 16
EC_FACTOR = 2
EXPERT_FF = 2816
LOG2E = math.log2(math.e)

LANES = 128
SEQ_PAD = 128
ZW = 2560
NQ, NK, NV, NU = 10, 9, 7, 8
FF_CHUNK = 256
MAX_KEY_CHUNK = 4352
COMBINE_WIN = 128
ROW_ALIGN = 16
FFN_ROWS = 2100
GATHER_ROWS = 704
NEG = -1e30
SAFE_LOGIT = 60.0
VMEM_LIMIT = 56 * 1024 * 1024


def _cparams(sem):
    return pltpu.CompilerParams(dimension_semantics=sem, vmem_limit_bytes=VMEM_LIMIT)


def _seg_rms(x, bd, inv_n):
    ss = jnp.dot((x * x).astype(BF16), bd, preferred_element_type=F32)
    return x * lax.rsqrt(ss * inv_n + EPS)


def _in_proj_kernel(h_ref, ln_ref, win_ref, wuq_ref, wkk_ref, wkv_ref, gv_ref, gcq_ref, rope_ref, bd_ref,
                    q_ref, k_ref, v_ref):
    h = h_ref[...]
    ms = jnp.mean(h * h, axis=-1, keepdims=True)
    u = (h * lax.rsqrt(ms + EPS) * ln_ref[...]).astype(BF16)
    z = jnp.dot(u, win_ref[...], preferred_element_type=F32)
    bd64, bd128, ones256 = bd_ref[0], bd_ref[1], bd_ref[2]

    def rope(x, t, sh):
        return (x * rope_ref[3 * t] + pltpu.roll(x, LANES - sh, 1) * rope_ref[3 * t + 1]
                + pltpu.roll(x, sh, 1) * rope_ref[3 * t + 2])

    def blk(x, j):
        return x[:, j * LANES:(j + 1) * LANES]

    for s in range(2):
        xq = _seg_rms(z[:, s * 256:(s + 1) * 256], bd64, 1.0 / DA)
        xk = _seg_rms(z[:, 512 + s * 256:512 + (s + 1) * 256], bd64, 1.0 / DA)
        for j in range(2):
            q_ref[2 * s + j] = rope(blk(xq, j) * gv_ref[0:1, :], 0, 32).astype(BF16)
            k_ref[2 * s + j] = rope(blk(xk, j) * gv_ref[1:2, :], 0, 32).astype(BF16)
    for j in range(4):
        v_ref[j] = blk(z, 8 + j).astype(BF16)

    cq = (_seg_rms(z[:, 1536:1792], ones256, 1.0 / Q_LORA) * gcq_ref[...]).astype(BF16)
    qb = jnp.dot(cq, wuq_ref[...], preferred_element_type=F32)
    ckv = (_seg_rms(z[:, 1792:1920], bd128[:LANES, :LANES], 1.0 / KV_LORA) * gv_ref[6:7, :]).astype(BF16)
    kk = jnp.dot(ckv, wkk_ref[...], preferred_element_type=F32)
    vb = jnp.dot(ckv, wkv_ref[...], preferred_element_type=F32)
    kr = pltpu.roll(z[:, 2432:2560], 64, 1)
    for s in range(2):
        xq = _seg_rms(qb[:, s * 256:(s + 1) * 256], bd128, 1.0 / (DB_NOPE + DB_ROPE))
        kraw = kk[:, s * 256:(s + 1) * 256] + jnp.concatenate([kr, kr], axis=1)
        xk = _seg_rms(kraw, bd128, 1.0 / (DB_NOPE + DB_ROPE))
        for j in range(2):
            q_ref[4 + 2 * s + j] = rope(blk(xq, j) * gv_ref[2:3, :], 1, 16).astype(BF16)
            k_ref[4 + 2 * s + j] = rope(blk(xk, j) * gv_ref[3:4, :], 1, 16).astype(BF16)
        v_ref[4 + s] = vb[:, s * LANES:(s + 1) * LANES].astype(BF16)

    xq = _seg_rms(z[:, 1920:2176], bd64, 1.0 / DC)
    for j in range(2):
        q_ref[8 + j] = rope(blk(xq, j) * gv_ref[4:5, :], 2, 16).astype(BF16)
    xk = _seg_rms(z[:, 2176:2304], bd64[:LANES, :LANES], 1.0 / DC)
    k_ref[8] = rope(xk * gv_ref[5:6, :], 2, 16).astype(BF16)
    v_ref[6] = z[:, 2304:2432].astype(BF16)


def _in_proj(h, ln, win, wuq, wkk, wkv, gv, gcq, rope, bd, *, tm, l_pad):
    n_pad = h.shape[0]
    tiles_per_seq = l_pad // tm
    full = lambda shape: pl.BlockSpec(shape, lambda i: (0,) * len(shape))
    return pl.pallas_call(
        _in_proj_kernel,
        grid=(n_pad // tm,),
        in_specs=[
            pl.BlockSpec((tm, D_MODEL), lambda i: (i, 0)),
            full((1, D_MODEL)), full((D_MODEL, ZW)), full((Q_LORA, 512)), full((KV_LORA, 512)),
            full((KV_LORA, 256)), full((8, LANES)), full((1, Q_LORA)),
            pl.BlockSpec((9, tm, LANES), lambda i: (0, i % tiles_per_seq, 0)),
            full((3, 256, 256)),
        ],
        out_specs=[
            pl.BlockSpec((NQ, tm, LANES), lambda i: (0, i, 0)),
            pl.BlockSpec((NK, tm, LANES), lambda i: (0, i, 0)),
            pl.BlockSpec((NV, tm, LANES), lambda i: (0, i, 0)),
        ],
        out_shape=[
            jax.ShapeDtypeStruct((NQ, n_pad, LANES), BF16),
            jax.ShapeDtypeStruct((NK, n_pad, LANES), BF16),
            jax.ShapeDtypeStruct((NV, n_pad, LANES), BF16),
        ],
        compiler_params=_cparams(("parallel",)),
        name="in_proj",
    )(h, ln, win, wuq, wkk, wkv, gv, gcq, rope, bd)


def _attn_kernel(tbl_ref, q0_ref, q1_ref, k0_ref, k1_ref, v_ref, up_ref, o_ref, *, n_chunks, tk, tq, l_real):
    up = up_ref[0]
    q0 = (q0_ref[0].astype(F32) * up[0:1, :]).astype(BF16)
    q1 = (q1_ref[0].astype(F32) * up[1:2, :]).astype(BF16)
    nt = (((1,), (1,)), ((), ()))
    last_mask = lax.broadcasted_iota(I32, (tq, tk), 1) < l_real - (n_chunks - 1) * tk

    def finish(a0, l0, a1, l1):
        o = up[2:3, :] * (a0 / l0) + up[3:4, :] * (a1 / l1)
        r = lax.rsqrt(jnp.mean(o * o, axis=-1, keepdims=True) + EPS)
        use = up[5:6, :]
        o_ref[...] = (o * (use * r + (1.0 - use)) * up[4:5, :]).astype(BF16)

    def chunks(step, init, unroll):
        if unroll:
            carry = init
            for c in range(n_chunks - 1):
                carry = step(c * tk, tk, carry, None)
        else:
            carry = lax.fori_loop(0, n_chunks - 1, lambda c, cr: step(pl.multiple_of(c * tk, tk), tk, cr, None),
                                  init)
        return step((n_chunks - 1) * tk, tk, carry, last_mask)

    @pl.when(tbl_ref[5, pl.program_id(1)] != 0)
    def _():
        def one(q, kc, vc, l, a, mask):
            s = lax.dot_general(q, kc, nt, preferred_element_type=F32)
            if mask is not None:
                s = jnp.where(mask, s, NEG)
            p = jnp.exp2(s)
            for j in range(p.shape[1] // LANES):
                l = l + p[:, j * LANES:(j + 1) * LANES]
            return l, a + jnp.dot(p.astype(BF16), vc, preferred_element_type=F32)

        def step(start, size, carry, mask):
            l0, a0, l1, a1 = carry
            vc = v_ref[0, 0, pl.ds(start, size), :]
            l0, a0 = one(q0, k0_ref[0, 0, pl.ds(start, size), :], vc, l0, a0, mask)
            l1, a1 = one(q1, k1_ref[0, 0, pl.ds(start, size), :], vc, l1, a1, mask)
            return l0, a0, l1, a1

        l0, a0, l1, a1 = chunks(step, (jnp.zeros((tq, LANES), F32),) * 4, True)
        finish(a0, jnp.sum(l0, axis=-1, keepdims=True), a1, jnp.sum(l1, axis=-1, keepdims=True))

    @pl.when(tbl_ref[5, pl.program_id(1)] == 0)
    def _():
        def one(q, kc, vc, m, l, a, mask):
            s = lax.dot_general(q, kc, nt, preferred_element_type=F32)
            if mask is not None:
                s = jnp.where(mask, s, NEG)
            mn = jnp.maximum(m, jnp.max(s, axis=-1, keepdims=True))
            al = jnp.exp2(m - mn)
            p = jnp.exp2(s - mn)
            l = al * l + jnp.sum(p, axis=-1, keepdims=True)
            a = al * a + jnp.dot(p.astype(BF16), vc, preferred_element_type=F32)
            return mn, l, a

        def step(start, size, carry, mask):
            m0, l0, a0, m1, l1, a1 = carry
            vc = v_ref[0, 0, pl.ds(start, size), :]
            m0, l0, a0 = one(q0, k0_ref[0, 0, pl.ds(start, size), :], vc, m0, l0, a0, mask)
            m1, l1, a1 = one(q1, k1_ref[0, 0, pl.ds(start, size), :], vc, m1, l1, a1, mask)
            return m0, l0, a0, m1, l1, a1

        init = (jnp.full((tq, 1), NEG, F32), jnp.zeros((tq, 1), F32), jnp.zeros((tq, LANES), F32)) * 2
        _, l0, a0, _, l1, a1 = chunks(step, init, False)
        finish(a0, l0, a1, l1)


def _attention(tbl, q, k, v, up, *, batch, l_pad, tq, n_chunks):
    n_pad = q.shape[1]
    nq = l_pad // tq
    tk = l_pad // n_chunks
    k4 = k.reshape(NK, batch, l_pad, LANES)
    v4 = v.reshape(NV, batch, l_pad, LANES)
    qspec = lambda row: pl.BlockSpec((1, tq, LANES), lambda b, u, i, t: (t[row, u], b * nq + i, 0))
    kspec = lambda row: pl.BlockSpec((1, 1, l_pad, LANES), lambda b, u, i, t: (t[row, u], b, 0, 0))
    grid_spec = pltpu.PrefetchScalarGridSpec(
        num_scalar_prefetch=1,
        grid=(batch, NU, nq),
        in_specs=[qspec(0), qspec(1), kspec(2), kspec(3), kspec(4),
                  pl.BlockSpec((1, 8, LANES), lambda b, u, i, t: (u, 0, 0))],
        out_specs=pl.BlockSpec((tq, LANES), lambda b, u, i, t: (b * nq + i, u)),
    )
    return pl.pallas_call(
        functools.partial(_attn_kernel, n_chunks=n_chunks, tk=tk, tq=tq, l_real=l_pad - SEQ_PAD + N_META),
        grid_spec=grid_spec,
        out_shape=jax.ShapeDtypeStruct((n_pad, NU * LANES), BF16),
        compiler_params=_cparams(("parallel", "parallel", "arbitrary")),
        name="attention",
    )(tbl, q, q, k4, k4, v4, up)


def _out_proj_kernel(mix_ref, h_ref, wout_ref, ln_ref, wrh_ref, wrl_ref, h1_ref, xt_ref, aff_ref, *, tm, l_real,
                     l_pad):
    h1 = h_ref[...] + jnp.dot(mix_ref[...], wout_ref[...], preferred_element_type=F32)
    h1_ref[...] = h1
    ms = jnp.mean(h1 * h1, axis=-1, keepdims=True)
    xt = h1 * lax.rsqrt(ms + EPS) * ln_ref[...]
    for j in range(D_MODEL // LANES):
        xt_ref[pl.ds(j, tm, stride=D_MODEL // LANES), :] = xt[:, j * LANES:(j + 1) * LANES]
    xh = xt.astype(BF16)
    xl = (xt - xh.astype(F32)).astype(BF16)
    lg = (jnp.dot(xh, wrh_ref[...], preferred_element_type=F32)
          + jnp.dot(xl, wrh_ref[...], preferred_element_type=F32)
          + jnp.dot(xh, wrl_ref[...], preferred_element_type=F32))
    lt = lg.T[:N_EXPERTS, :]
    mx = jnp.max(lt, axis=0, keepdims=True)
    ex = jnp.exp(lt - mx)
    aff = ex / jnp.sum(ex, axis=0, keepdims=True)
    row0 = (pl.program_id(0) % (l_pad // tm)) * tm
    row = row0 + lax.broadcasted_iota(I32, (N_EXPERTS, tm), 1)
    aff_ref[...] = jnp.where(row < l_real, aff, -1.0)


def _out_proj(mix, h, wout, ln, wrh, wrl, *, tm, l_real, l_pad):
    n_pad = h.shape[0]
    full = lambda shape: pl.BlockSpec(shape, lambda i: (0,) * len(shape))
    return pl.pallas_call(
        functools.partial(_out_proj_kernel, tm=tm, l_real=l_real, l_pad=l_pad),
        grid=(n_pad // tm,),
        in_specs=[
            pl.BlockSpec((tm, D_MODEL), lambda i: (i, 0)),
            pl.BlockSpec((tm, D_MODEL), lambda i: (i, 0)),
            full((D_MODEL, D_MODEL)), full((1, D_MODEL)), full((D_MODEL, LANES)), full((D_MODEL, LANES)),
        ],
        out_specs=[
            pl.BlockSpec((tm, D_MODEL), lambda i: (i, 0)),
            pl.BlockSpec((tm * (D_MODEL // LANES), LANES), lambda i: (i, 0)),
            pl.BlockSpec((N_EXPERTS, tm), lambda i: (0, i)),
        ],
        out_shape=[
            jax.ShapeDtypeStruct((n_pad, D_MODEL), F32),
            jax.ShapeDtypeStruct((n_pad * (D_MODEL // LANES), LANES), F32),
            jax.ShapeDtypeStruct((N_EXPERTS, n_pad), F32),
        ],
        compiler_params=_cparams(("parallel",)),
        name="out_proj",
    )(mix, h, wout, ln, wrh, wrl)


def _select_kernel(aff_ref, pos_ref, off_ref, idx_ref, *, cap, nb, n_chunks, n_tok):
    li = lax.broadcasted_iota(I32, (LANES, LANES), 0)
    lj = lax.broadcasted_iota(I32, (LANES, LANES), 1)
    upper_incl = (li <= lj).astype(BF16)
    ones_l = jnp.ones((LANES, LANES), BF16)
    bi = lax.broadcasted_iota(I32, (nb, nb), 0)
    bj = lax.broadcasted_iota(I32, (nb, nb), 1)
    lower_strict = (bj < bi).astype(BF16)
    upper_strict = (bi < bj).astype(BF16)
    ones8 = jnp.ones((8, LANES), BF16)
    blk_lane = lax.broadcasted_iota(I32, (LANES, nb), 1)
    slot_sub = lax.broadcasted_iota(I32, (LANES, nb), 0)
    slot_sub_l = lax.broadcasted_iota(I32, (LANES, LANES), 0)

    def prefix(mask_bf):
        pre = jnp.dot(mask_bf, upper_incl, preferred_element_type=F32)
        tot = jnp.dot(mask_bf, ones_l, preferred_element_type=F32)
        off = jnp.dot(lower_strict, tot.astype(BF16), preferred_element_type=F32)
        return pre, tot, off

    def per_expert(e, carry):
        a = aff_ref[e]
        bits = pltpu.bitcast(a, I32)

        def search(i, t):
            cand = t | lax.shift_left(jnp.int32(1), 30 - i)
            cnt = jnp.sum((bits >= cand).astype(I32), axis=(0, 1), keepdims=True)
            return jnp.where(cnt >= cap, cand, t)

        thr = lax.fori_loop(0, 31, search, jnp.zeros((1, 1), I32))
        gt = bits > thr
        eq = bits == thr
        need = (cap - jnp.sum(gt.astype(I32), axis=(0, 1), keepdims=True)).astype(F32)
        pre_e, _, off_e = prefix(jnp.where(eq, 1.0, 0.0).astype(BF16))
        sel = gt | (eq & (pre_e + off_e <= need))
        sel_bf = jnp.where(sel, 1.0, 0.0).astype(BF16)
        pre, tot, off = prefix(sel_bf)
        pos_ref[e] = jnp.where(sel, (pre + off).astype(I32) - 1, -1)
        off_ref[e] = off.astype(I32)

        tot_row = lax.dot_general(ones8, sel_bf, (((1,), (1,)), ((), ())), preferred_element_type=F32)
        off_row = jnp.dot(tot_row.astype(BF16), upper_strict, preferred_element_type=F32)[0:1, :]
        nxt_row = off_row + tot_row[0:1, :]
        pre_bf = pre.astype(BF16)

        def per_chunk(c, carry2):
            r = slot_sub + c * LANES
            nfull = jnp.sum((nxt_row <= r.astype(F32)).astype(I32), axis=1, keepdims=True)
            hot = blk_lane == nfull
            rows = jnp.dot(jnp.where(hot, 1.0, 0.0).astype(BF16), pre_bf, preferred_element_type=F32)
            base = jnp.sum(jnp.where(hot, off_row, 0.0), axis=1, keepdims=True)
            rem = (slot_sub_l[:, 0:1] + c * LANES).astype(F32) - base
            part = jnp.sum((rows <= rem).astype(I32), axis=1, keepdims=True)
            tok = jnp.minimum(nfull * LANES + part, n_tok - 1)
            idx_ref[e, c] = jnp.broadcast_to(tok, (LANES, LANES)).T[:8, :]
            return carry2

        lax.fori_loop(0, n_chunks, per_chunk, 0, unroll=2)
        return carry

    lax.fori_loop(0, N_EXPERTS, per_expert, 0)


def _select(aff3, *, cap, cap_pad, n_tok):
    nb = aff3.shape[1]
    n_chunks = 2 * (-(-cap_pad // (2 * LANES)))
    return pl.pallas_call(
        functools.partial(_select_kernel, cap=cap, nb=nb, n_chunks=n_chunks, n_tok=n_tok),
        out_shape=[
            jax.ShapeDtypeStruct((N_EXPERTS, nb, LANES), I32),
            jax.ShapeDtypeStruct((N_EXPERTS, nb, LANES), I32),
            jax.ShapeDtypeStruct((N_EXPERTS, n_chunks, 8, LANES), I32),
        ],
        compiler_params=pltpu.CompilerParams(vmem_limit_bytes=VMEM_LIMIT),
        name="select",
    )(aff3)


def _gather_kernel(idx_ref, nxt_ref, x_hbm, o_hbm, buf, obuf, gsem, wsem, *, rows, n_steps):
    s = pl.program_id(0)
    slot = s % 2
    other = 1 - slot
    sub = D_MODEL // LANES

    def start_rows(ids_ref, dst_slot):
        def start(g, c):
            for j in range(8):
                i = g * 8 + j
                src = x_hbm.at[pl.ds(pl.multiple_of(ids_ref[0, 0, i] * sub, sub), sub)]
                pltpu.make_async_copy(src, buf.at[dst_slot, pl.ds(i * sub, sub)],
                                      gsem.at[dst_slot]).start(priority=j % 2)
            return c
        lax.fori_loop(0, rows // 8, start, 0)

    def writeback(src_slot, step):
        return pltpu.make_async_copy(obuf.at[src_slot], o_hbm.at[pl.ds(step * rows, rows)], wsem.at[src_slot])

    @pl.when(s == 0)
    def _():
        start_rows(idx_ref, slot)

    @pl.when(s + 1 < n_steps)
    def _():
        start_rows(nxt_ref, other)

    pltpu.make_async_copy(x_hbm.at[pl.ds(0, rows * sub)], buf.at[slot], gsem.at[slot]).wait()

    @pl.when(s >= 2)
    def _():
        writeback(slot, s - 2).wait()

    for j in range(sub):
        obuf[slot, :, j * LANES:(j + 1) * LANES] = buf[slot, pl.ds(j, rows, stride=sub), :].astype(BF16)
    writeback(slot, s).start()

    @pl.when(s == n_steps - 1)
    def _():
        writeback(slot, s).wait()

    @pl.when(jnp.logical_and(s == n_steps - 1, n_steps >= 2))
    def _():
        writeback(other, s - 1).wait()


def _gather(idx, xt, *, rows):
    e, cap_pad = idx.shape
    n_steps = e * cap_pad // rows
    sub = D_MODEL // LANES
    xt3 = xt.reshape(-1, LANES)
    ids = idx.reshape(n_steps, 1, rows)
    out = pl.pallas_call(
        functools.partial(_gather_kernel, rows=rows, n_steps=n_steps),
        grid=(n_steps,),
        in_specs=[
            pl.BlockSpec((1, 1, rows), lambda s: (s, 0, 0), memory_space=pltpu.SMEM),
            pl.BlockSpec((1, 1, rows), lambda s: (jnp.minimum(s + 1, n_steps - 1), 0, 0), memory_space=pltpu.SMEM),
            pl.BlockSpec(memory_space=pl.ANY),
        ],
        out_specs=pl.BlockSpec(memory_space=pl.ANY),
        out_shape=jax.ShapeDtypeStruct((e * cap_pad, D_MODEL), BF16),
        scratch_shapes=[pltpu.VMEM((2, rows * sub, LANES), F32), pltpu.VMEM((2, rows, D_MODEL), BF16),
                        pltpu.SemaphoreType.DMA((2,)), pltpu.SemaphoreType.DMA((2,))],
        compiler_params=_cparams(("arbitrary",)),
        name="gather",
    )(ids, ids, xt3)
    return out.reshape(e, cap_pad, D_MODEL)


def _ffn_kernel(x_ref, wg_ref, wu_ref, wd_ref, o_ref, acc_ref):
    f = pl.program_id(2)

    @pl.when(f == 0)
    def _():
        acc_ref[...] = jnp.zeros_like(acc_ref)

    x = x_ref[0]
    g = jnp.dot(x, wg_ref[0].astype(BF16), preferred_element_type=F32)
    u = jnp.dot(x, wu_ref[0].astype(BF16), preferred_element_type=F32)
    hid = (g / (1.0 + jnp.exp(-g)) * u).astype(BF16)
    acc_ref[...] += jnp.dot(hid, wd_ref[0].astype(BF16), preferred_element_type=F32)

    @pl.when(f == pl.num_programs(2) - 1)
    def _():
        o_ref[0] = acc_ref[...].astype(BF16)


def _ffn(xe, w_gate, w_up, w_down, *, layer, rt):
    e, cap_pad = xe.shape[:2]
    return pl.pallas_call(
        _ffn_kernel,
        grid=(e, cap_pad // rt, EXPERT_FF // FF_CHUNK),
        in_specs=[
            pl.BlockSpec((1, rt, D_MODEL), lambda i, r, f: (i, r, 0)),
            pl.BlockSpec((None, 1, D_MODEL, FF_CHUNK), lambda i, r, f: (layer, i, 0, f)),
            pl.BlockSpec((None, 1, D_MODEL, FF_CHUNK), lambda i, r, f: (layer, i, 0, f)),
            pl.BlockSpec((None, 1, FF_CHUNK, D_MODEL), lambda i, r, f: (layer, i, f, 0)),
        ],
        out_specs=pl.BlockSpec((1, rt, D_MODEL), lambda i, r, f: (i, r, 0)),
        out_shape=jax.ShapeDtypeStruct((e, cap_pad, D_MODEL), BF16),
        scratch_shapes=[pltpu.VMEM((rt, D_MODEL), F32)],
        compiler_params=_cparams(("parallel", "parallel", "arbitrary")),
        name="ffn",
    )(xe, w_gate, w_up, w_down)


def _combine_kernel(off_ref, pos_ref, aff_ref, ye_hbm, h_ref, o_ref, *rest, tile, seq):
    if seq is None:
        yw_ref, st_ref, sem = rest
    else:
        y_hbm, yw_ref, st_ref, sem, ysem = rest
    t = pl.program_id(0)
    n_tiles = pl.num_programs(0)
    buf = t % 2

    @pl.when(t == 0)
    def _():
        yw_ref[...] = jnp.zeros_like(yw_ref)

    o_ref[...] = h_ref[...]

    last_start = ye_hbm.shape[1] - COMBINE_WIN

    def window(tt, e, k):
        base = off_ref[e, tt]
        end = off_ref[e, tt + 1]
        first = lax.shift_left(lax.shift_right_logical(base, 4), 4) + k * COMBINE_WIN
        return first, jnp.minimum(first, last_start), jnp.logical_and(end > base, end > first)

    def copy(e, start, b):
        return pltpu.make_async_copy(ye_hbm.at[e, pl.ds(pl.multiple_of(start, 16), COMBINE_WIN), :],
                                     yw_ref.at[b, pl.ds(e * COMBINE_WIN, COMBINE_WIN), :], sem.at[b])

    def start_round(tt, k, b):
        for e in range(N_EXPERTS):
            _, start, active = window(tt, e, k)
            pl.when(active)(copy(e, start, b).start)

    slot = lax.broadcasted_iota(I32, (COMBINE_WIN, tile), 0)

    def finish_round(k):
        for e in range(N_EXPERTS):
            first, start, active = window(t, e, k)
            start = jnp.where(active, start, -(1 << 30))
            pos = pos_ref[e:e + 1, :]
            pos = jnp.where(pos >= first, pos, -1)
            st_ref[e * COMBINE_WIN:(e + 1) * COMBINE_WIN, :] = jnp.where(
                pos - start == slot, aff_ref[e:e + 1, :], 0.0).astype(BF16)
        for e in range(N_EXPERTS):
            _, start, active = window(t, e, k)
            pl.when(active)(copy(e, start, buf).wait)
        o_ref[...] += lax.dot_general(st_ref[...], yw_ref[buf], (((0,), (0,)), ((), ())),
                                      preferred_element_type=F32)

    @pl.when(t == 0)
    def _():
        start_round(t, 0, buf)

    @pl.when(t + 1 < n_tiles)
    def _():
        start_round(t + 1, 0, 1 - buf)

    finish_round(0)

    n_rounds = jnp.int32(0)
    for e in range(N_EXPERTS):
        first, _, _ = window(t, e, 0)
        span = jnp.where(off_ref[e, t + 1] > off_ref[e, t], off_ref[e, t + 1] - first, 0)
        n_rounds = jnp.maximum(n_rounds, lax.shift_right_logical(span + (COMBINE_WIN - 1), 7))

    def extra_round(k, c):
        start_round(t, k, buf)
        finish_round(k)
        return c

    lax.fori_loop(1, n_rounds, extra_round, 0)

    if seq is not None:
        l_pad, s_real = seq

        def piece(p):
            row = t * tile + p * ROW_ALIGN
            b = row // l_pad
            l = row - b * l_pad
            dst = pl.multiple_of(jnp.maximum(b * s_real + l - N_META, 0), ROW_ALIGN)
            cp = pltpu.make_async_copy(o_ref.at[pl.ds(p * ROW_ALIGN, ROW_ALIGN), :],
                                       y_hbm.at[pl.ds(dst, ROW_ALIGN), :], ysem)
            return jnp.logical_and(l >= N_META, l < N_META + s_real), cp

        for p in range(tile // ROW_ALIGN):
            real, cp = piece(p)
            pl.when(real)(cp.start)
        for p in range(tile // ROW_ALIGN):
            real, cp = piece(p)
            pl.when(real)(cp.wait)


def _combine(off, pos, aff, ye, h1, *, tile, seq=None):
    n_pad = h1.shape[0]
    e = ye.shape[0]
    h_spec = pl.BlockSpec((tile, D_MODEL), lambda t, o: (t, 0))
    out_specs, out_shape = h_spec, jax.ShapeDtypeStruct((n_pad, D_MODEL), F32)
    scratch = [pltpu.VMEM((2, e * COMBINE_WIN, D_MODEL), BF16), pltpu.VMEM((e * COMBINE_WIN, tile), BF16),
               pltpu.SemaphoreType.DMA((2,))]
    if seq is not None:
        l_pad, s_real = seq
        assert l_pad % ROW_ALIGN == 0 and s_real % ROW_ALIGN == 0 and N_META % ROW_ALIGN == 0
        out_specs = [h_spec, pl.BlockSpec(memory_space=pl.ANY)]
        out_shape = [out_shape, jax.ShapeDtypeStruct((n_pad // l_pad * s_real, D_MODEL), F32)]
        scratch = scratch + [pltpu.SemaphoreType.DMA(())]
    grid_spec = pltpu.PrefetchScalarGridSpec(
        num_scalar_prefetch=1,
        grid=(n_pad // tile,),
        in_specs=[
            pl.BlockSpec((e, tile), lambda t, o: (0, t)),
            pl.BlockSpec((e, tile), lambda t, o: (0, t)),
            pl.BlockSpec(memory_space=pl.ANY),
            h_spec,
        ],
        out_specs=out_specs,
        scratch_shapes=scratch,
    )
    return pl.pallas_call(
        functools.partial(_combine_kernel, tile=tile, seq=seq),
        grid_spec=grid_spec,
        out_shape=out_shape,
        compiler_params=_cparams(("arbitrary",)),
        name="combine",
    )(off, pos, aff, ye, h1)


def _rope_tables(s, l_pad):
    del s
    l = np.arange(l_pad)
    real = l - N_META
    pos = l.astype(np.float64)
    row = np.where(l < N_META, -1, real // GRID_W).astype(np.float64)
    col = np.where(l < N_META, l, real % GRID_W).astype(np.float64)

    def cos_sin(p, half):
        ang = p[:, None] * (ROPE_THETA ** (-np.arange(half, dtype=np.float64) / half))[None, :]
        return jnp.asarray(np.cos(ang), F32), jnp.asarray(np.sin(ang), F32)

    def tables(groups):
        c, s1, s2 = [], [], []
        for g in groups:
            if isinstance(g, int):
                c.append(jnp.ones((l_pad, g), F32))
                s1.append(jnp.zeros((l_pad, g), F32))
                s2.append(jnp.zeros((l_pad, g), F32))
            else:
                co, si = g
                z = jnp.zeros_like(si)
                c += [co, co]
                s1 += [-si, z]
                s2 += [z, si]
        return [jnp.concatenate(t, axis=1) for t in (c, s1, s2)]

    a = cos_sin(pos, 32)
    b = cos_sin(pos, 16)
    rr, cc = cos_sin(row, 16), cos_sin(col, 16)
    return jnp.stack(tables([a, a]) + tables([64, b, 32]) + tables([rr, cc, rr, cc]))


def _block_diag(seg):
    i = np.arange(256)
    return (i[:, None] // seg == i[None, :] // seg).astype(np.float32)


def _prep_layer(layer, w_in, w_out, g_qa, g_ka, lam_q1, lam_k1, lam_q2, lam_k2, g_suba, g_cq, w_uq, g_ckv, w_ukv,
                g_qb, g_kb, g_qc, g_kc, w_router):
    zq = w_in[:, 1952:2208].reshape(D_MODEL, HC_KV, HC // HC_KV, DC).transpose(0, 2, 1, 3).reshape(D_MODEL, 256)
    kr = jnp.pad(w_in[:, 1920:1952], ((0, 0), (0, LANES - DB_ROPE)))
    win = jnp.concatenate([w_in[:, :1920], zq, w_in[:, 2208:2464], kr], axis=1).astype(BF16)
    wuq = jnp.pad(w_uq.reshape(Q_LORA, HB, DB_NOPE + DB_ROPE), ((0, 0), (0, 0), (0, 32))).reshape(Q_LORA, 512)
    ukv = w_ukv.reshape(KV_LORA, HB, DB_NOPE + DB_V)
    wkk = jnp.pad(ukv[:, :, :DB_NOPE], ((0, 0), (0, 0), (0, LANES - DB_NOPE))).reshape(KV_LORA, 512)
    wkv = ukv[:, :, DB_NOPE:].reshape(KV_LORA, HB * DB_V)
    oc = w_out[768:].reshape(HC_KV, HC // HC_KV, DC, D_MODEL).transpose(1, 0, 2, 3).reshape(256, D_MODEL)
    wout = jnp.concatenate([w_out[:768], oc], axis=0).astype(BF16)

    pad96 = lambda g: jnp.pad(g, (0, LANES - DB_NOPE - DB_ROPE))
    gv = jnp.stack([
        jnp.tile(g_qa, 2) * (DA ** -0.5 * LOG2E), jnp.tile(g_ka, 2),
        pad96(g_qb) * ((DB_NOPE + DB_ROPE) ** -0.5 * LOG2E), pad96(g_kb),
        jnp.tile(g_qc, 2) * (DC ** -0.5 * LOG2E), jnp.tile(g_kc, 2),
        g_ckv, jnp.zeros((LANES,), F32)]).astype(F32)

    lam_init = 0.8 - 0.6 * math.exp(-0.3 * layer)
    lam = jnp.exp(jnp.sum(lam_q1 * lam_k1)) - jnp.exp(jnp.sum(lam_q2 * lam_k2)) + lam_init
    lo = (jnp.arange(LANES) < 64).astype(F32)
    hi = 1.0 - lo
    one = jnp.ones((LANES,), F32)
    zero = jnp.zeros((LANES,), F32)
    unit_a = jnp.stack([lo, hi, one, -lam * one, g_suba * (1.0 - lam_init), one, zero, zero])
    unit_b = jnp.stack([one, one, lo, hi, one, zero, zero, zero])
    unit_c = jnp.stack([lo, hi, lo, hi, one, zero, zero, zero])
    up = jnp.stack([unit_a] * 4 + [unit_b] * 2 + [unit_c] * 2).astype(F32)

    wr = jnp.pad(w_router, ((0, 0), (0, LANES - N_EXPERTS)))
    wrh = wr.astype(BF16)
    wrl = (wr - wrh.astype(F32)).astype(BF16)
    def bounded(n, gq, gk):
        return n * jnp.max(jnp.abs(gq)) * jnp.max(jnp.abs(gk)) <= SAFE_LOGIT

    flags = jnp.stack([bounded(DA, gv[0], gv[1])] * HA + [bounded(DB_NOPE + DB_ROPE, gv[2], gv[3])] * 2
                      + [bounded(DC, gv[4], gv[5])] * 2).astype(I32)
    tbl = jnp.concatenate([jnp.asarray(_UNIT_TABLE), flags[None, :]], axis=0)
    return dict(win=win, wuq=wuq.astype(BF16), wkk=wkk.astype(BF16), wkv=wkv.astype(BF16), wout=wout, gv=gv,
                gcq=g_cq.reshape(1, Q_LORA).astype(F32), up=up, wrh=wrh, wrl=wrl, tbl=tbl)


_UNIT_TABLE = np.array([
    [0, 1, 2, 3, 4, 6, 8, 9],
    [0, 1, 2, 3, 5, 7, 8, 9],
    [0, 1, 2, 3, 4, 6, 8, 8],
    [0, 1, 2, 3, 5, 7, 8, 8],
    [0, 1, 2, 3, 4, 5, 6, 6],
], np.int32)


def _tiles(s):
    l_pad = s + SEQ_PAD
    tm = max(t for t in (384, 640, 128) if l_pad % t == 0)
    blocks = l_pad // LANES
    n_chunks = min(n for n in range(1, blocks + 1) if blocks % n == 0 and l_pad // n <= MAX_KEY_CHUNK)
    return tm, n_chunks


def _route_tiles(cap):
    n_rt = max(1, round(cap / FFN_ROWS))
    rt = max(-(-(-(-cap // n_rt)) // ROW_ALIGN) * ROW_ALIGN, COMBINE_WIN)
    cap_pad = n_rt * rt
    g_rows = max(r for r in range(ROW_ALIGN, GATHER_ROWS + 1, ROW_ALIGN) if cap_pad % r == 0)
    return rt, cap_pad, g_rows


def _encode(x, meta_tokens, ln_mix, ln_ffn, layers, w_gate, w_up, w_down):
    b, s, d = x.shape
    l_real = s + N_META
    l_pad = s + SEQ_PAD
    n_pad = b * l_pad
    tm, n_chunks = _tiles(s)
    cap = EC_FACTOR * (b * l_real) // N_EXPERTS
    ctile = max(t for t in (256, 128) if n_pad % t == 0)
    rt, cap_pad, g_rows = _route_tiles(cap)
    nb_pad = -(-(n_pad // LANES) // LANES) * LANES

    h = jnp.concatenate([jnp.broadcast_to(meta_tokens[None], (b, N_META, d)), x,
                         jnp.zeros((b, SEQ_PAD - N_META, d), x.dtype)], axis=1).reshape(n_pad, d)
    rope = _rope_tables(s, l_pad)
    bd = jnp.asarray(np.stack([_block_diag(64), _block_diag(128), _block_diag(256)]), BF16)

    for l, p in enumerate(layers):
        q, k, v = _in_proj(h, ln_mix[l].reshape(1, d), p["win"], p["wuq"], p["wkk"], p["wkv"], p["gv"], p["gcq"],
                           rope, bd, tm=tm, l_pad=l_pad)
        mix = _attention(p["tbl"], q, k, v, p["up"], batch=b, l_pad=l_pad, tq=tm, n_chunks=n_chunks)
        h1, xt, aff = _out_proj(mix, h, p["wout"], ln_ffn[l].reshape(1, d), p["wrh"], p["wrl"], tm=tm,
                                l_real=l_real, l_pad=l_pad)
        aff3 = jnp.pad(aff, ((0, 0), (0, nb_pad * LANES - n_pad)), constant_values=-1.0)
        pos3, off3, idxb = _select(aff3.reshape(N_EXPERTS, nb_pad, LANES), cap=cap, cap_pad=cap_pad, n_tok=n_pad)
        idx = idxb[:, :, 0, :].reshape(N_EXPERTS, -1)[:, :cap_pad]
        pos = pos3.reshape(N_EXPERTS, nb_pad * LANES)[:, :n_pad]
        off = off3[:, ::ctile // LANES, 0][:, :n_pad // ctile + 1]
        xe = _gather(idx, xt, rows=g_rows)
        ye = _ffn(xe, w_gate, w_up, w_down, layer=l, rt=rt)
        if l + 1 < len(layers):
            h = _combine(off, pos, aff, ye, h1, tile=ctile)
        else:
            _, y = _combine(off, pos, aff, ye, h1, tile=ctile, seq=(l_pad, s))
    return y.reshape(b, s, d)


def kernel(x_prompt, x_sample, meta_tokens, ln_mix, w_in, w_out, g_qa, g_ka, lam_q1, lam_k1, lam_q2, lam_k2, g_suba,
           g_cq, w_uq, g_ckv, w_ukv, g_qb, g_kb, g_qc, g_kc, ln_ffn, w_router, w_gate, w_up, w_down):
    depth = w_in.shape[0]
    layers = [
        _prep_layer(l, w_in[l], w_out[l], g_qa[l], g_ka[l], lam_q1[l], lam_k1[l], lam_q2[l], lam_k2[l], g_suba[l],
                    g_cq[l], w_uq[l], g_ckv[l], w_ukv[l], g_qb[l], g_kb[l], g_qc[l], g_kc[l], w_router[l])
        for l in range(depth)
    ]
    y_prompt = _encode(x_prompt, meta_tokens, ln_mix, ln_ffn, layers, w_gate, w_up, w_down)
    y_sample = _encode(x_sample, meta_tokens, ln_mix, ln_ffn, layers, w_gate, w_up, w_down)
    return (y_prompt, y_sample)
```

```python
import functools
import math

import jax
import jax.numpy as jnp
import numpy as np
from jax import lax
from jax.experimental import pallas as pl
from jax.experimental.pallas import tpu as pltpu

F32 = jnp.float32
BF16 = jnp.bfloat16
I32 = jnp.int32

D_MODEL = 1024
N_META = 16
GRID_W = 64
ROPE_THETA = 10000.0
EPS = 1e-6
HA, DA, VA = 4, 64, 128
HB, Q_LORA, KV_LORA, DB_NOPE, DB_ROPE, DB_V = 4, 256, 128, 64, 32, 64
HC, HC_KV, DC = 4, 2, 64
N_EXPERTS = 16
EC_FACTOR = 2
EXPERT_FF = 2816
LOG2E = math.log2(math.e)

LANES = 128
SEQ_PAD = 128
ZW = 2560
NQ, NK, NV, NU = 10, 9, 7, 8
FF_CHUNK = 256
MAX_KEY_CHUNK = 4352
COMBINE_WIN = 128
ROW_ALIGN = 16
FFN_ROWS = 2100
GATHER_ROWS = 704
NEG = -1e30
SAFE_LOGIT = 60.0
VMEM_LIMIT = 56 * 1024 * 1024


def _cparams(sem):
    return pltpu.CompilerParams(dimension_semantics=sem, vmem_limit_bytes=VMEM_LIMIT)


def _seg_rms(x, bd, inv_n):
    ss = jnp.dot((x * x).astype(BF16), bd, preferred_element_type=F32)
    return x * lax.rsqrt(ss * inv_n + EPS)


def _in_proj_kernel(h_ref, ln_ref, win_ref, wuq_ref, wkk_ref, wkv_ref, gv_ref, gcq_ref, rope_ref, bd_ref,
                    q_ref, k_ref, v_ref):
    h = h_ref[...]
    ms = jnp.mean(h * h, axis=-1, keepdims=True)
    u = (h * lax.rsqrt(ms + EPS) * ln_ref[...]).astype(BF16)
    z = jnp.dot(u, win_ref[...], preferred_element_type=F32)
    bd64, bd128, ones256 = bd_ref[0], bd_ref[1], bd_ref[2]

    def rope(x, t, sh):
        return (x * rope_ref[3 * t] + pltpu.roll(x, LANES - sh, 1) * rope_ref[3 * t + 1]
                + pltpu.roll(x, sh, 1) * rope_ref[3 * t + 2])

    def blk(x, j):
        return x[:, j * LANES:(j + 1) * LANES]

    for s in range(2):
        xq = _seg_rms(z[:, s * 256:(s + 1) * 256], bd64, 1.0 / DA)
        xk = _seg_rms(z[:, 512 + s * 256:512 + (s + 1) * 256], bd64, 1.0 / DA)
        for j in range(2):
            q_ref[2 * s + j] = rope(blk(xq, j) * gv_ref[0:1, :], 0, 32).astype(BF16)
            k_ref[2 * s + j] = rope(blk(xk, j) * gv_ref[1:2, :], 0, 32).astype(BF16)
    for j in range(4):
        v_ref[j] = blk(z, 8 + j).astype(BF16)

    cq = (_seg_rms(z[:, 1536:1792], ones256, 1.0 / Q_LORA) * gcq_ref[...]).astype(BF16)
    qb = jnp.dot(cq, wuq_ref[...], preferred_element_type=F32)
    ckv = (_seg_rms(z[:, 1792:1920], bd128[:LANES, :LANES], 1.0 / KV_LORA) * gv_ref[6:7, :]).astype(BF16)
    kk = jnp.dot(ckv, wkk_ref[...], preferred_element_type=F32)
    vb = jnp.dot(ckv, wkv_ref[...], preferred_element_type=F32)
    kr = pltpu.roll(z[:, 2432:2560], 64, 1)
    for s in range(2):
        xq = _seg_rms(qb[:, s * 256:(s + 1) * 256], bd128, 1.0 / (DB_NOPE + DB_ROPE))
        kraw = kk[:, s * 256:(s + 1) * 256] + jnp.concatenate([kr, kr], axis=1)
        xk = _seg_rms(kraw, bd128, 1.0 / (DB_NOPE + DB_ROPE))
        for j in range(2):
            q_ref[4 + 2 * s + j] = rope(blk(xq, j) * gv_ref[2:3, :], 1, 16).astype(BF16)
            k_ref[4 + 2 * s + j] = rope(blk(xk, j) * gv_ref[3:4, :], 1, 16).astype(BF16)
        v_ref[4 + s] = vb[:, s * LANES:(s + 1) * LANES].astype(BF16)

    xq = _seg_rms(z[:, 1920:2176], bd64, 1.0 / DC)
    for j in range(2):
        q_ref[8 + j] = rope(blk(xq, j) * gv_ref[4:5, :], 2, 16).astype(BF16)
    xk = _seg_rms(z[:, 2176:2304], bd64[:LANES, :LANES], 1.0 / DC)
    k_ref[8] = rope(xk * gv_ref[5:6, :], 2, 16).astype(BF16)
    v_ref[6] = z[:, 2304:2432].astype(BF16)


def _in_proj(h, ln, win, wuq, wkk, wkv, gv, gcq, rope, bd, *, tm, l_pad):
    n_pad = h.shape[0]
    tiles_per_seq = l_pad // tm
    full = lambda shape: pl.BlockSpec(shape, lambda i: (0,) * len(shape))
    return pl.pallas_call(
        _in_proj_kernel,
        grid=(n_pad // tm,),
        in_specs=[
            pl.BlockSpec((tm, D_MODEL), lambda i: (i, 0)),
            full((1, D_MODEL)), full((D_MODEL, ZW)), full((Q_LORA, 512)), full((KV_LORA, 512)),
            full((KV_LORA, 256)), full((8, LANES)), full((1, Q_LORA)),
            pl.BlockSpec((9, tm, LANES), lambda i: (0, i % tiles_per_seq, 0)),
            full((3, 256, 256)),
        ],
        out_specs=[
            pl.BlockSpec((NQ, tm, LANES), lambda i: (0, i, 0)),
            pl.BlockSpec((NK, tm, LANES), lambda i: (0, i, 0)),
            pl.BlockSpec((NV, tm, LANES), lambda i: (0, i, 0)),
        ],
        out_shape=[
            jax.ShapeDtypeStruct((NQ, n_pad, LANES), BF16),
            jax.ShapeDtypeStruct((NK, n_pad, LANES), BF16),
            jax.ShapeDtypeStruct((NV, n_pad, LANES), BF16),
        ],
        compiler_params=_cparams(("parallel",)),
        name="in_proj",
    )(h, ln, win, wuq, wkk, wkv, gv, gcq, rope, bd)


def _attn_kernel(tbl_ref, q0_ref, q1_ref, k0_ref, k1_ref, v_ref, up_ref, o_ref, *, n_chunks, tk, tq, l_real):
    up = up_ref[0]
    q0 = (q0_ref[0].astype(F32) * up[0:1, :]).astype(BF16)
    q1 = (q1_ref[0].astype(F32) * up[1:2, :]).astype(BF16)
    nt = (((1,), (1,)), ((), ()))
    last_mask = lax.broadcasted_iota(I32, (tq, tk), 1) < l_real - (n_chunks - 1) * tk

    def finish(a0, l0, a1, l1):
        o = up[2:3, :] * (a0 / l0) + up[3:4, :] * (a1 / l1)
        r = lax.rsqrt(jnp.mean(o * o, axis=-1, keepdims=True) + EPS)
        use = up[5:6, :]
        o_ref[...] = (o * (use * r + (1.0 - use)) * up[4:5, :]).astype(BF16)

    def chunks(step, init, unroll):
        if unroll:
            carry = init
            for c in range(n_chunks - 1):
                carry = step(c * tk, tk, carry, None)
        else:
            carry = lax.fori_loop(0, n_chunks - 1, lambda c, cr: step(pl.multiple_of(c * tk, tk), tk, cr, None),
                                  init)
        return step((n_chunks - 1) * tk, tk, carry, last_mask)

    @pl.when(tbl_ref[5, pl.program_id(1)] != 0)
    def _():
        def one(q, kc, vc, l, a, mask):
            s = lax.dot_general(q, kc, nt, preferred_element_type=F32)
            if mask is not None:
                s = jnp.where(mask, s, NEG)
            p = jnp.exp2(s)
            for j in range(p.shape[1] // LANES):
                l = l + p[:, j * LANES:(j + 1) * LANES]
            return l, a + jnp.dot(p.astype(BF16), vc, preferred_element_type=F32)

        def step(start, size, carry, mask):
            l0, a0, l1, a1 = carry
            vc = v_ref[0, 0, pl.ds(start, size), :]
            l0, a0 = one(q0, k0_ref[0, 0, pl.ds(start, size), :], vc, l0, a0, mask)
            l1, a1 = one(q1, k1_ref[0, 0, pl.ds(start, size), :], vc, l1, a1, mask)
            return l0, a0, l1, a1

        l0, a0, l1, a1 = chunks(step, (jnp.zeros((tq, LANES), F32),) * 4, True)
        finish(a0, jnp.sum(l0, axis=-1, keepdims=True), a1, jnp.sum(l1, axis=-1, keepdims=True))

    @pl.when(tbl_ref[5, pl.program_id(1)] == 0)
    def _():
        def one(q, kc, vc, m, l, a, mask):
            s = lax.dot_general(q, kc, nt, preferred_element_type=F32)
            if mask is not None:
                s = jnp.where(mask, s, NEG)
            mn = jnp.maximum(m, jnp.max(s, axis=-1, keepdims=True))
            al = jnp.exp2(m - mn)
            p = jnp.exp2(s - mn)
            l = al * l + jnp.sum(p, axis=-1, keepdims=True)
            a = al * a + jnp.dot(p.astype(BF16), vc, preferred_element_type=F32)
            return mn, l, a

        def step(start, size, carry, mask):
            m0, l0, a0, m1, l1, a1 = carry
            vc = v_ref[0, 0, pl.ds(start, size), :]
            m0, l0, a0 = one(q0, k0_ref[0, 0, pl.ds(start, size), :], vc, m0, l0, a0, mask)
            m1, l1, a1 = one(q1, k1_ref[0, 0, pl.ds(start, size), :], vc, m1, l1, a1, mask)
            return m0, l0, a0, m1, l1, a1

        init = (jnp.full((tq, 1), NEG, F32), jnp.zeros((tq, 1), F32), jnp.zeros((tq, LANES), F32)) * 2
        _, l0, a0, _, l1, a1 = chunks(step, init, False)
        finish(a0, l0, a1, l1)


def _attention(tbl, q, k, v, up, *, batch, l_pad, tq, n_chunks):
    n_pad = q.shape[1]
    nq = l_pad // tq
    tk = l_pad // n_chunks
    k4 = k.reshape(NK, batch, l_pad, LANES)
    v4 = v.reshape(NV, batch, l_pad, LANES)
    qspec = lambda row: pl.BlockSpec((1, tq, LANES), lambda b, u, i, t: (t[row, u], b * nq + i, 0))
    kspec = lambda row: pl.BlockSpec((1, 1, l_pad, LANES), lambda b, u, i, t: (t[row, u], b, 0, 0))
    grid_spec = pltpu.PrefetchScalarGridSpec(
        num_scalar_prefetch=1,
        grid=(batch, NU, nq),
        in_specs=[qspec(0), qspec(1), kspec(2), kspec(3), kspec(4),
                  pl.BlockSpec((1, 8, LANES), lambda b, u, i, t: (u, 0, 0))],
        out_specs=pl.BlockSpec((tq, LANES), lambda b, u, i, t: (b * nq + i, u)),
    )
    return pl.pallas_call(
        functools.partial(_attn_kernel, n_chunks=n_chunks, tk=tk, tq=tq, l_real=l_pad - SEQ_PAD + N_META),
        grid_spec=grid_spec,
        out_shape=jax.ShapeDtypeStruct((n_pad, NU * LANES), BF16),
        compiler_params=_cparams(("parallel", "parallel", "arbitrary")),
        name="attention",
    )(tbl, q, q, k4, k4, v4, up)


def _out_proj_kernel(mix_ref, h_ref, wout_ref, ln_ref, wrh_ref, wrl_ref, h1_ref, xt_ref, aff_ref, *, tm, l_real,
                     l_pad):
    h1 = h_ref[...] + jnp.dot(mix_ref[...], wout_ref[...], preferred_element_type=F32)
    h1_ref[...] = h1
    ms = jnp.mean(h1 * h1, axis=-1, keepdims=True)
    xt = h1 * lax.rsqrt(ms + EPS) * ln_ref[...]
    for j in range(D_MODEL // LANES):
        xt_ref[pl.ds(j, tm, stride=D_MODEL // LANES), :] = xt[:, j * LANES:(j + 1) * LANES]
    xh = xt.astype(BF16)
    xl = (xt - xh.astype(F32)).astype(BF16)
    lg = (jnp.dot(xh, wrh_ref[...], preferred_element_type=F32)
          + jnp.dot(xl, wrh_ref[...], preferred_element_type=F32)
          + jnp.dot(xh, wrl_ref[...], preferred_element_type=F32))
    lt = lg.T[:N_EXPERTS, :]
    mx = jnp.max(lt, axis=0, keepdims=True)
    ex = jnp.exp(lt - mx)
    aff = ex / jnp.sum(ex, axis=0, keepdims=True)
    row0 = (pl.program_id(0) % (l_pad // tm)) * tm
    row = row0 + lax.broadcasted_iota(I32, (N_EXPERTS, tm), 1)
    aff_ref[...] = jnp.where(row < l_real, aff, -1.0)


def _out_proj(mix, h, wout, ln, wrh, wrl, *, tm, l_real, l_pad):
    n_pad = h.shape[0]
    full = lambda shape: pl.BlockSpec(shape, lambda i: (0,) * len(shape))
    return pl.pallas_call(
        functools.partial(_out_proj_kernel, tm=tm, l_real=l_real, l_pad=l_pad),
        grid=(n_pad // tm,),
        in_specs=[
            pl.BlockSpec((tm, D_MODEL), lambda i: (i, 0)),
            pl.BlockSpec((tm, D_MODEL), lambda i: (i, 0)),
            full((D_MODEL, D_MODEL)), full((1, D_MODEL)), full((D_MODEL, LANES)), full((D_MODEL, LANES)),
        ],
        out_specs=[
            pl.BlockSpec((tm, D_MODEL), lambda i: (i, 0)),
            pl.BlockSpec((tm * (D_MODEL // LANES), LANES), lambda i: (i, 0)),
            pl.BlockSpec((N_EXPERTS, tm), lambda i: (0, i)),
        ],
        out_shape=[
            jax.ShapeDtypeStruct((n_pad, D_MODEL), F32),
            jax.ShapeDtypeStruct((n_pad * (D_MODEL // LANES), LANES), F32),
            jax.ShapeDtypeStruct((N_EXPERTS, n_pad), F32),
        ],
        compiler_params=_cparams(("parallel",)),
        name="out_proj",
    )(mix, h, wout, ln, wrh, wrl)


def _select_kernel(aff_ref, pos_ref, off_ref, idx_ref, thr_ref, *, cap, nb, n_chunks, n_tok):
    li = lax.broadcasted_iota(I32, (LANES, LANES), 0)
    lj = lax.broadcasted_iota(I32, (LANES, LANES), 1)
    upper_incl = (li <= lj).astype(BF16)
    ones_l = jnp.ones((LANES, LANES), BF16)
    bi = lax.broadcasted_iota(I32, (nb, nb), 0)
    bj = lax.broadcasted_iota(I32, (nb, nb), 1)
    lower_strict = (bj < bi).astype(BF16)
    upper_strict = (bi < bj).astype(BF16)
    ones8 = jnp.ones((8, LANES), BF16)
    blk_lane = lax.broadcasted_iota(I32, (LANES, nb), 1)
    slot_sub = lax.broadcasted_iota(I32, (LANES, nb), 0)
    slot_sub_l = lax.broadcasted_iota(I32, (LANES, LANES), 0)

    def prefix(mask_bf):
        pre = jnp.dot(mask_bf, upper_incl, preferred_element_type=F32)
        tot = jnp.dot(mask_bf, ones_l, preferred_element_type=F32)
        off = jnp.dot(lower_strict, tot.astype(BF16), preferred_element_type=F32)
        return pre, tot, off

    def search(i, t):
        cand = t | lax.shift_left(jnp.int32(1), 30 - i)
        bits_all = pltpu.bitcast(aff_ref[...], I32)
        cnt = jnp.sum((bits_all >= cand).astype(I32), axis=(1, 2), keepdims=True)
        return jnp.where(cnt >= cap, cand, t)

    thr_all = lax.fori_loop(0, 31, search, jnp.zeros((N_EXPERTS, 1, 1), I32))
    thr_ref[...] = jnp.broadcast_to(thr_all, thr_ref.shape)

    def per_expert(e, carry):
        a = aff_ref[e]
        bits = pltpu.bitcast(a, I32)
        thr = thr_ref[e][0:1, 0:1]
        gt = bits > thr
        eq = bits == thr
        need = (cap - jnp.sum(gt.astype(I32), axis=(0, 1), keepdims=True)).astype(F32)
        pre_e, _, off_e = prefix(jnp.where(eq, 1.0, 0.0).astype(BF16))
        sel = gt | (eq & (pre_e + off_e <= need))
        sel_bf = jnp.where(sel, 1.0, 0.0).astype(BF16)
        pre, tot, off = prefix(sel_bf)
        pos_ref[e] = jnp.where(sel, (pre + off).astype(I32) - 1, -1)
        off_ref[e] = off.astype(I32)

        tot_row = lax.dot_general(ones8, sel_bf, (((1,), (1,)), ((), ())), preferred_element_type=F32)
        off_row = jnp.dot(tot_row.astype(BF16), upper_strict, preferred_element_type=F32)[0:1, :]
        nxt_row = off_row + tot_row[0:1, :]
        pre_bf = pre.astype(BF16)

        def per_chunk(c, carry2):
            r = slot_sub + c * LANES
            nfull = jnp.sum((nxt_row <= r.astype(F32)).astype(I32), axis=1, keepdims=True)
            hot = blk_lane == nfull
            rows = jnp.dot(jnp.where(hot, 1.0, 0.0).astype(BF16), pre_bf, preferred_element_type=F32)
            base = jnp.sum(jnp.where(hot, off_row, 0.0), axis=1, keepdims=True)
            rem = (slot_sub_l[:, 0:1] + c * LANES).astype(F32) - base
            part = jnp.sum((rows <= rem).astype(I32), axis=1, keepdims=True)
            tok = jnp.minimum(nfull * LANES + part, n_tok - 1)
            idx_ref[e, c] = jnp.broadcast_to(tok, (LANES, LANES)).T[:8, :]
            return carry2

        lax.fori_loop(0, n_chunks, per_chunk, 0, unroll=2)
        return carry

    lax.fori_loop(0, N_EXPERTS, per_expert, 0)


def _select(aff3, *, cap, cap_pad, n_tok):
    nb = aff3.shape[1]
    n_chunks = 2 * (-(-cap_pad // (2 * LANES)))
    return pl.pallas_call(
        functools.partial(_select_kernel, cap=cap, nb=nb, n_chunks=n_chunks, n_tok=n_tok),
        out_shape=[
            jax.ShapeDtypeStruct((N_EXPERTS, nb, LANES), I32),
            jax.ShapeDtypeStruct((N_EXPERTS, nb, LANES), I32),
            jax.ShapeDtypeStruct((N_EXPERTS, n_chunks, 8, LANES), I32),
        ],
        scratch_shapes=[pltpu.VMEM((N_EXPERTS, 8, LANES), I32)],
        compiler_params=pltpu.CompilerParams(vmem_limit_bytes=VMEM_LIMIT),
        name="select",
    )(aff3)


def _gather_kernel(idx_ref, nxt_ref, x_hbm, o_hbm, buf, obuf, gsem, wsem, *, rows, n_steps):
    s = pl.program_id(0)
    slot = s % 2
    other = 1 - slot
    sub = D_MODEL // LANES

    def start_rows(ids_ref, dst_slot):
        def start(g, c):
            for j in range(8):
                i = g * 8 + j
                src = x_hbm.at[pl.ds(pl.multiple_of(ids_ref[0, 0, i] * sub, sub), sub)]
                pltpu.make_async_copy(src, buf.at[dst_slot, pl.ds(i * sub, sub)],
                                      gsem.at[dst_slot]).start(priority=j % 2)
            return c
        lax.fori_loop(0, rows // 8, start, 0)

    def writeback(src_slot, step):
        return pltpu.make_async_copy(obuf.at[src_slot], o_hbm.at[pl.ds(step * rows, rows)], wsem.at[src_slot])

    @pl.when(s == 0)
    def _():
        start_rows(idx_ref, slot)

    @pl.when(s + 1 < n_steps)
    def _():
        start_rows(nxt_ref, other)

    pltpu.make_async_copy(x_hbm.at[pl.ds(0, rows * sub)], buf.at[slot], gsem.at[slot]).wait()

    @pl.when(s >= 2)
    def _():
        writeback(slot, s - 2).wait()

    for j in range(sub):
        obuf[slot, :, j * LANES:(j + 1) * LANES] = buf[slot, pl.ds(j, rows, stride=sub), :].astype(BF16)
    writeback(slot, s).start()

    @pl.when(s == n_steps - 1)
    def _():
        writeback(slot, s).wait()

    @pl.when(jnp.logical_and(s == n_steps - 1, n_steps >= 2))
    def _():
        writeback(other, s - 1).wait()


def _gather(idx, xt, *, rows):
    e, cap_pad = idx.shape
    n_steps = e * cap_pad // rows
    sub = D_MODEL // LANES
    xt3 = xt.reshape(-1, LANES)
    ids = idx.reshape(n_steps, 1, rows)
    out = pl.pallas_call(
        functools.partial(_gather_kernel, rows=rows, n_steps=n_steps),
        grid=(n_steps,),
        in_specs=[
            pl.BlockSpec((1, 1, rows), lambda s: (s, 0, 0), memory_space=pltpu.SMEM),
            pl.BlockSpec((1, 1, rows), lambda s: (jnp.minimum(s + 1, n_steps - 1), 0, 0), memory_space=pltpu.SMEM),
            pl.BlockSpec(memory_space=pl.ANY),
        ],
        out_specs=pl.BlockSpec(memory_space=pl.ANY),
        out_shape=jax.ShapeDtypeStruct((e * cap_pad, D_MODEL), BF16),
        scratch_shapes=[pltpu.VMEM((2, rows * sub, LANES), F32), pltpu.VMEM((2, rows, D_MODEL), BF16),
                        pltpu.SemaphoreType.DMA((2,)), pltpu.SemaphoreType.DMA((2,))],
        compiler_params=_cparams(("arbitrary",)),
        name="gather",
    )(ids, ids, xt3)
    return out.reshape(e, cap_pad, D_MODEL)


def _ffn_kernel(x_ref, wg_ref, wu_ref, wd_ref, o_ref, acc_ref):
    f = pl.program_id(2)

    @pl.when(f == 0)
    def _():
        acc_ref[...] = jnp.zeros_like(acc_ref)

    x = x_ref[0]
    g = jnp.dot(x, wg_ref[0].astype(BF16), preferred_element_type=F32)
    u = jnp.dot(x, wu_ref[0].astype(BF16), preferred_element_type=F32)
    hid = (g / (1.0 + jnp.exp(-g)) * u).astype(BF16)
    acc_ref[...] += jnp.dot(hid, wd_ref[0].astype(BF16), preferred_element_type=F32)

    @pl.when(f == pl.num_programs(2) - 1)
    def _():
        o_ref[0] = acc_ref[...].astype(BF16)


def _ffn(xe, w_gate, w_up, w_down, *, layer, rt):
    e, cap_pad = xe.shape[:2]
    return pl.pallas_call(
        _ffn_kernel,
        grid=(e, cap_pad // rt, EXPERT_FF // FF_CHUNK),
        in_specs=[
            pl.BlockSpec((1, rt, D_MODEL), lambda i, r, f: (i, r, 0)),
            pl.BlockSpec((None, 1, D_MODEL, FF_CHUNK), lambda i, r, f: (layer, i, 0, f)),
            pl.BlockSpec((None, 1, D_MODEL, FF_CHUNK), lambda i, r, f: (layer, i, 0, f)),
            pl.BlockSpec((None, 1, FF_CHUNK, D_MODEL), lambda i, r, f: (layer, i, f, 0)),
        ],
        out_specs=pl.BlockSpec((1, rt, D_MODEL), lambda i, r, f: (i, r, 0)),
        out_shape=jax.ShapeDtypeStruct((e, cap_pad, D_MODEL), BF16),
        scratch_shapes=[pltpu.VMEM((rt, D_MODEL), F32)],
        compiler_params=_cparams(("parallel", "parallel", "arbitrary")),
        name="ffn",
    )(xe, w_gate, w_up, w_down)


def _combine_kernel(off_ref, pos_ref, aff_ref, ye_hbm, h_ref, o_ref, *rest, tile, seq):
    if seq is None:
        yw_ref, st_ref, sem = rest
    else:
        y_hbm, yw_ref, st_ref, sem, ysem = rest
    t = pl.program_id(0)
    n_tiles = pl.num_programs(0)
    buf = t % 2

    @pl.when(t == 0)
    def _():
        yw_ref[...] = jnp.zeros_like(yw_ref)

    o_ref[...] = h_ref[...]

    last_start = ye_hbm.shape[1] - COMBINE_WIN

    def window(tt, e, k):
        base = off_ref[e, tt]
        end = off_ref[e, tt + 1]
        first = lax.shift_left(lax.shift_right_logical(base, 4), 4) + k * COMBINE_WIN
        return first, jnp.minimum(first, last_start), jnp.logical_and(end > base, end > first)

    def copy(e, start, b):
        return pltpu.make_async_copy(ye_hbm.at[e, pl.ds(pl.multiple_of(start, 16), COMBINE_WIN), :],
                                     yw_ref.at[b, pl.ds(e * COMBINE_WIN, COMBINE_WIN), :], sem.at[b])

    def start_round(tt, k, b):
        for e in range(N_EXPERTS):
            _, start, active = window(tt, e, k)
            pl.when(active)(copy(e, start, b).start)

    slot = lax.broadcasted_iota(I32, (COMBINE_WIN, tile), 0)

    def finish_round(k):
        for e in range(N_EXPERTS):
            first, start, active = window(t, e, k)
            start = jnp.where(active, start, -(1 << 30))
            pos = pos_ref[e:e + 1, :]
            pos = jnp.where(pos >= first, pos, -1)
            st_ref[e * COMBINE_WIN:(e + 1) * COMBINE_WIN, :] = jnp.where(
                pos - start == slot, aff_ref[e:e + 1, :], 0.0).astype(BF16)
        for e in range(N_EXPERTS):
            _, start, active = window(t, e, k)
            pl.when(active)(copy(e, start, buf).wait)
        o_ref[...] += lax.dot_general(st_ref[...], yw_ref[buf], (((0,), (0,)), ((), ())),
                                      preferred_element_type=F32)

    @pl.when(t == 0)
    def _():
        start_round(t, 0, buf)

    @pl.when(t + 1 < n_tiles)
    def _():
        start_round(t + 1, 0, 1 - buf)

    finish_round(0)

    n_rounds = jnp.int32(0)
    for e in range(N_EXPERTS):
        first, _, _ = window(t, e, 0)
        span = jnp.where(off_ref[e, t + 1] > off_ref[e, t], off_ref[e, t + 1] - first, 0)
        n_rounds = jnp.maximum(n_rounds, lax.shift_right_logical(span + (COMBINE_WIN - 1), 7))

    def extra_round(k, c):
        start_round(t, k, buf)
        finish_round(k)
        return c

    lax.fori_loop(1, n_rounds, extra_round, 0)

    if seq is not None:
        l_pad, s_real = seq

        def piece(p):
            row = t * tile + p * ROW_ALIGN
            b = row // l_pad
            l = row - b * l_pad
            dst = pl.multiple_of(jnp.maximum(b * s_real + l - N_META, 0), ROW_ALIGN)
            cp = pltpu.make_async_copy(o_ref.at[pl.ds(p * ROW_ALIGN, ROW_ALIGN), :],
                                       y_hbm.at[pl.ds(dst, ROW_ALIGN), :], ysem)
            return jnp.logical_and(l >= N_META, l < N_META + s_real), cp

        for p in range(tile // ROW_ALIGN):
            real, cp = piece(p)
            pl.when(real)(cp.start)
        for p in range(tile // ROW_ALIGN):
            real, cp = piece(p)
            pl.when(real)(cp.wait)


def _combine(off, pos, aff, ye, h1, *, tile, seq=None):
    n_pad = h1.shape[0]
    e = ye.shape[0]
    h_spec = pl.BlockSpec((tile, D_MODEL), lambda t, o: (t, 0))
    out_specs, out_shape = h_spec, jax.ShapeDtypeStruct((n_pad, D_MODEL), F32)
    scratch = [pltpu.VMEM((2, e * COMBINE_WIN, D_MODEL), BF16), pltpu.VMEM((e * COMBINE_WIN, tile), BF16),
               pltpu.SemaphoreType.DMA((2,))]
    if seq is not None:
        l_pad, s_real = seq
        assert l_pad % ROW_ALIGN == 0 and s_real % ROW_ALIGN == 0 and N_META % ROW_ALIGN == 0
        out_specs = [h_spec, pl.BlockSpec(memory_space=pl.ANY)]
        out_shape = [out_shape, jax.ShapeDtypeStruct((n_pad // l_pad * s_real, D_MODEL), F32)]
        scratch = scratch + [pltpu.SemaphoreType.DMA(())]
    grid_spec = pltpu.PrefetchScalarGridSpec(
        num_scalar_prefetch=1,
        grid=(n_pad // tile,),
        in_specs=[
            pl.BlockSpec((e, tile), lambda t, o: (0, t)),
            pl.BlockSpec((e, tile), lambda t, o: (0, t)),
            pl.BlockSpec(memory_space=pl.ANY),
            h_spec,
        ],
        out_specs=out_specs,
        scratch_shapes=scratch,
    )
    return pl.pallas_call(
        functools.partial(_combine_kernel, tile=tile, seq=seq),
        grid_spec=grid_spec,
        out_shape=out_shape,
        compiler_params=_cparams(("arbitrary",)),
        name="combine",
    )(off, pos, aff, ye, h1)


def _rope_tables(s, l_pad):
    del s
    l = np.arange(l_pad)
    real = l - N_META
    pos = l.astype(np.float64)
    row = np.where(l < N_META, -1, real // GRID_W).astype(np.float64)
    col = np.where(l < N_META, l, real % GRID_W).astype(np.float64)

    def cos_sin(p, half):
        ang = p[:, None] * (ROPE_THETA ** (-np.arange(half, dtype=np.float64) / half))[None, :]
        return jnp.asarray(np.cos(ang), F32), jnp.asarray(np.sin(ang), F32)

    def tables(groups):
        c, s1, s2 = [], [], []
        for g in groups:
            if isinstance(g, int):
                c.append(jnp.ones((l_pad, g), F32))
                s1.append(jnp.zeros((l_pad, g), F32))
                s2.append(jnp.zeros((l_pad, g), F32))
            else:
                co, si = g
                z = jnp.zeros_like(si)
                c += [co, co]
                s1 += [-si, z]
                s2 += [z, si]
        return [jnp.concatenate(t, axis=1) for t in (c, s1, s2)]

    a = cos_sin(pos, 32)
    b = cos_sin(pos, 16)
    rr, cc = cos_sin(row, 16), cos_sin(col, 16)
    return jnp.stack(tables([a, a]) + tables([64, b, 32]) + tables([rr, cc, rr, cc]))


def _block_diag(seg):
    i = np.arange(256)
    return (i[:, None] // seg == i[None, :] // seg).astype(np.float32)


def _prep_layer(layer, w_in, w_out, g_qa, g_ka, lam_q1, lam_k1, lam_q2, lam_k2, g_suba, g_cq, w_uq, g_ckv, w_ukv,
                g_qb, g_kb, g_qc, g_kc, w_router):
    zq = w_in[:, 1952:2208].reshape(D_MODEL, HC_KV, HC // HC_KV, DC).transpose(0, 2, 1, 3).reshape(D_MODEL, 256)
    kr = jnp.pad(w_in[:, 1920:1952], ((0, 0), (0, LANES - DB_ROPE)))
    win = jnp.concatenate([w_in[:, :1920], zq, w_in[:, 2208:2464], kr], axis=1).astype(BF16)
    wuq = jnp.pad(w_uq.reshape(Q_LORA, HB, DB_NOPE + DB_ROPE), ((0, 0), (0, 0), (0, 32))).reshape(Q_LORA, 512)
    ukv = w_ukv.reshape(KV_LORA, HB, DB_NOPE + DB_V)
    wkk = jnp.pad(ukv[:, :, :DB_NOPE], ((0, 0), (0, 0), (0, LANES - DB_NOPE))).reshape(KV_LORA, 512)
    wkv = ukv[:, :, DB_NOPE:].reshape(KV_LORA, HB * DB_V)
    oc = w_out[768:].reshape(HC_KV, HC // HC_KV, DC, D_MODEL).transpose(1, 0, 2, 3).reshape(256, D_MODEL)
    wout = jnp.concatenate([w_out[:768], oc], axis=0).astype(BF16)

    pad96 = lambda g: jnp.pad(g, (0, LANES - DB_NOPE - DB_ROPE))
    gv = jnp.stack([
        jnp.tile(g_qa, 2) * (DA ** -0.5 * LOG2E), jnp.tile(g_ka, 2),
        pad96(g_qb) * ((DB_NOPE + DB_ROPE) ** -0.5 * LOG2E), pad96(g_kb),
        jnp.tile(g_qc, 2) * (DC ** -0.5 * LOG2E), jnp.tile(g_kc, 2),
        g_ckv, jnp.zeros((LANES,), F32)]).astype(F32)

    lam_init = 0.8 - 0.6 * math.exp(-0.3 * layer)
    lam = jnp.exp(jnp.sum(lam_q1 * lam_k1)) - jnp.exp(jnp.sum(lam_q2 * lam_k2)) + lam_init
    lo = (jnp.arange(LANES) < 64).astype(F32)
    hi = 1.0 - lo
    one = jnp.ones((LANES,), F32)
    zero = jnp.zeros((LANES,), F32)
    unit_a = jnp.stack([lo, hi, one, -lam * one, g_suba * (1.0 - lam_init), one, zero, zero])
    unit_b = jnp.stack([one, one, lo, hi, one, zero, zero, zero])
    unit_c = jnp.stack([lo, hi, lo, hi, one, zero, zero, zero])
    up = jnp.stack([unit_a] * 4 + [unit_b] * 2 + [unit_c] * 2).astype(F32)

    wr = jnp.pad(w_router, ((0, 0), (0, LANES - N_EXPERTS)))
    wrh = wr.astype(BF16)
    wrl = (wr - wrh.astype(F32)).astype(BF16)
    def bounded(n, gq, gk):
        return n * jnp.max(jnp.abs(gq)) * jnp.max(jnp.abs(gk)) <= SAFE_LOGIT

    flags = jnp.stack([bounded(DA, gv[0], gv[1])] * HA + [bounded(DB_NOPE + DB_ROPE, gv[2], gv[3])] * 2
                      + [bounded(DC, gv[4], gv[5])] * 2).astype(I32)
    tbl = jnp.concatenate([jnp.asarray(_UNIT_TABLE), flags[None, :]], axis=0)
    return dict(win=win, wuq=wuq.astype(BF16), wkk=wkk.astype(BF16), wkv=wkv.astype(BF16), wout=wout, gv=gv,
                gcq=g_cq.reshape(1, Q_LORA).astype(F32), up=up, wrh=wrh, wrl=wrl, tbl=tbl)


_UNIT_TABLE = np.array([
    [0, 1, 2, 3, 4, 6, 8, 9],
    [0, 1, 2, 3, 5, 7, 8, 9],
    [0, 1, 2, 3, 4, 6, 8, 8],
    [0, 1, 2, 3, 5, 7, 8, 8],
    [0, 1, 2, 3, 4, 5, 6, 6],
], np.int32)


def _tiles(s):
    l_pad = s + SEQ_PAD
    tm = max(t for t in (384, 640, 128) if l_pad % t == 0)
    blocks = l_pad // LANES
    n_chunks = min(n for n in range(1, blocks + 1) if blocks % n == 0 and l_pad // n <= MAX_KEY_CHUNK)
    return tm, n_chunks


def _route_tiles(cap):
    n_rt = max(1, round(cap / FFN_ROWS))
    rt = max(-(-(-(-cap // n_rt)) // ROW_ALIGN) * ROW_ALIGN, COMBINE_WIN)
    cap_pad = n_rt * rt
    g_rows = max(r for r in range(ROW_ALIGN, GATHER_ROWS + 1, ROW_ALIGN) if cap_pad % r == 0)
    return rt, cap_pad, g_rows


def _encode(x, meta_tokens, ln_mix, ln_ffn, layers, w_gate, w_up, w_down):
    b, s, d = x.shape
    l_real = s + N_META
    l_pad = s + SEQ_PAD
    n_pad = b * l_pad
    tm, n_chunks = _tiles(s)
    cap = EC_FACTOR * (b * l_real) // N_EXPERTS
    ctile = max(t for t in (256, 128) if n_pad % t == 0)
    rt, cap_pad, g_rows = _route_tiles(cap)
    nb_pad = -(-(n_pad // LANES) // LANES) * LANES

    h = jnp.concatenate([jnp.broadcast_to(meta_tokens[None], (b, N_META, d)), x,
                         jnp.zeros((b, SEQ_PAD - N_META, d), x.dtype)], axis=1).reshape(n_pad, d)
    rope = _rope_tables(s, l_pad)
    bd = jnp.asarray(np.stack([_block_diag(64), _block_diag(128), _block_diag(256)]), BF16)

    for l, p in enumerate(layers):
        q, k, v = _in_proj(h, ln_mix[l].reshape(1, d), p["win"], p["wuq"], p["wkk"], p["wkv"], p["gv"], p["gcq"],
                           rope, bd, tm=tm, l_pad=l_pad)
        mix = _attention(p["tbl"], q, k, v, p["up"], batch=b, l_pad=l_pad, tq=tm, n_chunks=n_chunks)
        h1, xt, aff = _out_proj(mix, h, p["wout"], ln_ffn[l].reshape(1, d), p["wrh"], p["wrl"], tm=tm,
                                l_real=l_real, l_pad=l_pad)
        aff3 = jnp.pad(aff, ((0, 0), (0, nb_pad * LANES - n_pad)), constant_values=-1.0)
        pos3, off3, idxb = _select(aff3.reshape(N_EXPERTS, nb_pad, LANES), cap=cap, cap_pad=cap_pad, n_tok=n_pad)
        idx = idxb[:, :, 0, :].reshape(N_EXPERTS, -1)[:, :cap_pad]
        pos = pos3.reshape(N_EXPERTS, nb_pad * LANES)[:, :n_pad]
        off = off3[:, ::ctile // LANES, 0][:, :n_pad // ctile + 1]
        xe = _gather(idx, xt, rows=g_rows)
        ye = _ffn(xe, w_gate, w_up, w_down, layer=l, rt=rt)
        if l + 1 < len(layers):
            h = _combine(off, pos, aff, ye, h1, tile=ctile)
        else:
            _, y = _combine(off, pos, aff, ye, h1, tile=ctile, seq=(l_pad, s))
    return y.reshape(b, s, d)


def kernel(x_prompt, x_sample, meta_tokens, ln_mix, w_in, w_out, g_qa, g_ka, lam_q1, lam_k1, lam_q2, lam_k2, g_suba,
           g_cq, w_uq, g_ckv, w_ukv, g_qb, g_kb, g_qc, g_kc, ln_ffn, w_router, w_gate, w_up, w_down):
    depth = w_in.shape[0]
    layers = [
        _prep_layer(l, w_in[l], w_out[l], g_qa[l], g_ka[l], lam_q1[l], lam_k1[l], lam_q2[l], lam_k2[l], g_suba[l],
                    g_cq[l], w_uq[l], g_ckv[l], w_ukv[l], g_qb[l], g_kb[l], g_qc[l], g_kc[l], w_router[l])
        for l in range(depth)
    ]
    y_prompt = _encode(x_prompt, meta_tokens, ln_mix, ln_ffn, layers, w_gate, w_up, w_down)
    y_sample = _encode(x_sample, meta_tokens, ln_mix, ln_ffn, layers, w_gate, w_up, w_down)
    return (y_prompt, y_sample)
```
